```python
import jax, jax.numpy as jnp
from jax import lax
import numpy as np

D_MODEL = 1024
BATCH = 8
SEQ = 4096
DEPTH = 2
DEC_BATCH = 1
DEC_SEQ = 16384
PAST_LEN = 128

EPS = 1e-6
N_BRANCH = 4
BRANCH_WIDTH = 256
A_WIDTH = BRANCH_WIDTH
A_GROUPS = 4
CONV_WIDTH = 3
POOL_WIDTH = BRANCH_WIDTH
POOL_WINDOWS = (2, 4, 8, 16)
POOL_GROUPS = len(POOL_WINDOWS)
POOL_GDIM = POOL_WIDTH // POOL_GROUPS
SG_WIDTH = BRANCH_WIDTH
SG_GROUPS = 4
SG_GDIM = SG_WIDTH // SG_GROUPS
CHUNK = 128
N_Q_HEADS = 4
N_KV_HEADS = 2
GQA_GROUP = N_Q_HEADS // N_KV_HEADS
HEAD_DIM = 64
Q_DIM = N_Q_HEADS * HEAD_DIM
KV_DIM = N_KV_HEADS * HEAD_DIM
WINDOW = 128
ATTN_BLOCK = 128
NEG_BIG = -1e30
IN_A = 3 * A_WIDTH
IN_POOL = POOL_WIDTH
IN_SG = 2 * SG_WIDTH
IN_ATTN = Q_DIM + 2 * KV_DIM
D_IN = IN_A + IN_POOL + IN_SG + IN_ATTN
D_FF = ((8 * D_MODEL + 3 * 256 - 1) // (3 * 256)) * 256

kernel_name = "hybrid_gated_parallel_encoder"


def rms_norm(x, gain):
    xf = x.astype(jnp.float32)
    y = xf * lax.rsqrt(jnp.mean(xf * xf, axis=-1, keepdims=True) + EPS)
    return (y * gain.astype(jnp.float32)).astype(x.dtype)


def layer_norm(x, gain):
    xf = x.astype(jnp.float32)
    mu = jnp.mean(xf, axis=-1, keepdims=True)
    xc = xf - mu
    y = xc * lax.rsqrt(jnp.mean(xc * xc, axis=-1, keepdims=True) + EPS)
    return (y * gain.astype(jnp.float32)).astype(x.dtype)


def alibi_slopes():
    return jnp.exp2(-8.0 * (jnp.arange(N_Q_HEADS, dtype=jnp.float32) + 1.0) / N_Q_HEADS)


def short_conv_mixer(z, conv_w):
    b_gate, c_gate, xin = jnp.split(z, 3, axis=-1)
    u = c_gate * xin
    s = u.shape[1]
    up = jnp.pad(u, ((0, 0), (1, 1), (0, 0)))
    conv = conv_w[0] * up[:, :s] + conv_w[1] * up[:, 1:s + 1] + conv_w[2] * up[:, 2:]
    return b_gate * conv


def pool_mixer(z, pool_w, pool_scale):
    b, s, _ = z.shape
    zf = z.astype(jnp.float32)
    cs = jnp.concatenate([jnp.zeros((b, 1, POOL_WIDTH), jnp.float32), jnp.cumsum(zf, axis=1)], axis=1)
    pos = jnp.arange(s)
    outs = []
    for g, w in enumerate(POOL_WINDOWS):
        lo = jnp.clip(pos - w // 2, 0, s)
        hi = jnp.clip(pos + w // 2, 0, s)
        csg = cs[..., g * POOL_GDIM:(g + 1) * POOL_GDIM]
        total = jnp.take(csg, hi, axis=1) - jnp.take(csg, lo, axis=1)
        cnt = (hi - lo).astype(jnp.float32)[None, :, None]
        outs.append(total / cnt)
    pooled = (jnp.concatenate(outs, axis=-1) - zf).astype(z.dtype).reshape(b, s, POOL_GROUPS, POOL_GDIM)
    mixed = jnp.einsum('bsgc,gcd->bsgd', pooled, pool_w).reshape(b, s, POOL_WIDTH)
    return mixed * pool_scale


def spatial_gating_mixer(z, sg_norm, sg_w, sg_b):
    b, s, _ = z.shape
    u, v = jnp.split(z, 2, axis=-1)
    v = layer_norm(v, sg_norm).reshape(b, s // CHUNK, CHUNK, SG_GROUPS, SG_GDIM)
    mixed = jnp.einsum('gts,bnsgc->bntgc', sg_w, v) + jnp.transpose(sg_b)[None, None, :, :, None]
    return u * mixed.reshape(b, s, SG_WIDTH)


def windowed_gqa(z, sink):
    b, s, _ = z.shape
    q, k, v = jnp.split(z, [Q_DIM, Q_DIM + KV_DIM], axis=-1)
    nb = s // ATTN_BLOCK
    q = q.reshape(b, nb, ATTN_BLOCK, N_KV_HEADS, GQA_GROUP, HEAD_DIM)

    def band(t):
        tp = jnp.pad(t, ((0, 0), (ATTN_BLOCK, ATTN_BLOCK), (0, 0)))
        tp = tp.reshape(b, nb + 2, ATTN_BLOCK, N_KV_HEADS, HEAD_DIM)
        return jnp.concatenate([tp[:, :-2], tp[:, 1:-1], tp[:, 2:]], axis=2)

    kb, vb = band(k), band(v)
    scores = jnp.einsum('bnqkgd,bnskd->bnkgqs', q, kb).astype(jnp.float32) * (HEAD_DIM ** -0.5)
    qi = jnp.arange(ATTN_BLOCK)[:, None]
    kj = jnp.arange(3 * ATTN_BLOCK)[None, :]
    rel = qi - kj + ATTN_BLOCK
    key_pos = jnp.arange(nb)[:, None] * ATTN_BLOCK - ATTN_BLOCK + kj
    valid = (jnp.abs(rel) <= WINDOW)[None] & ((key_pos >= 0) & (key_pos < s))[:, None, :]
    slopes = alibi_slopes().reshape(N_KV_HEADS, GQA_GROUP)
    alibi = -slopes[:, :, None, None] * jnp.abs(rel).astype(jnp.float32)[None, None]
    scores = jnp.where(valid[None, :, None, None], scores + alibi[None, None], NEG_BIG)
    sink_l = sink.astype(jnp.float32).reshape(N_KV_HEADS, GQA_GROUP)[None, None, :, :, None, None]
    m = jnp.maximum(jnp.max(scores, axis=-1, keepdims=True), sink_l)
    p = jnp.exp(scores - m)
    denom = jnp.sum(p, axis=-1, keepdims=True) + jnp.exp(sink_l - m)
    probs = (p / denom).astype(vb.dtype)
    out = jnp.einsum('bnkgqs,bnskd->bnqkgd', probs, vb)
    return out.reshape(b, s, Q_DIM)


def trunk(x, norm_mix_pre, norm_mix_post, norm_ffn_pre, norm_ffn_post, w_in, conv_w,
          pool_w, pool_scale, sg_norm, sg_w, sg_b, attn_sink, w_branch, w_gate, b_gate,
          w_out, w_ffn_in, w_ffn_out):
    for l in range(DEPTH):
        h = rms_norm(x, norm_mix_pre[l])
        z = h @ w_in[l]
        z_a, z_p, z_s, z_d = jnp.split(z, [IN_A, IN_A + IN_POOL, IN_A + IN_POOL + IN_SG], axis=-1)
        branches = (
            short_conv_mixer(z_a, conv_w[l]),
            pool_mixer(z_p, pool_w[l], pool_scale[l]),
            spatial_gating_mixer(z_s, sg_norm[l], sg_w[l], sg_b[l]),
            windowed_gqa(z_d, attn_sink[l]),
        )
        merged = None
        for i, br in enumerate(branches):
            gate = jax.nn.sigmoid(h @ w_gate[l, i] + b_gate[l, i])
            term = gate * (br @ w_branch[l, i])
            merged = term if merged is None else merged + term
        x = x + rms_norm(merged @ w_out[l], norm_mix_post[l])
        h = rms_norm(x, norm_ffn_pre[l])
        a, g = jnp.split(h @ w_ffn_in[l], 2, axis=-1)
        x = x + rms_norm((jax.nn.silu(a) * g) @ w_ffn_out[l], norm_ffn_post[l])
    return x


def setup_inputs(seed: int = 0) -> dict:
    key = jax.random.key(seed)
    ks = jax.random.split(key, 24)
    f32 = jnp.float32

    def nrm(k, shape, scale):
        return jax.random.normal(k, shape, f32) * scale

    def gain(k, shape):
        return 1.0 + 0.1 * jax.random.normal(k, shape, f32)

    return {
        "x_prompt": nrm(ks[0], (BATCH, SEQ, D_MODEL), 1.0),
        "x_sample": nrm(ks[1], (DEC_BATCH, DEC_SEQ, D_MODEL), 1.0),
        "norm_mix_pre": gain(ks[2], (DEPTH, D_MODEL)),
        "norm_mix_post": gain(ks[3], (DEPTH, D_MODEL)),
        "norm_ffn_pre": gain(ks[4], (DEPTH, D_MODEL)),
        "norm_ffn_post": gain(ks[5], (DEPTH, D_MODEL)),
        "w_in": nrm(ks[6], (DEPTH, D_MODEL, D_IN), D_MODEL ** -0.5),
        "conv_w": nrm(ks[7], (DEPTH, CONV_WIDTH, A_WIDTH), CONV_WIDTH ** -0.5),
        "pool_w": nrm(ks[8], (DEPTH, POOL_GROUPS, POOL_GDIM, POOL_GDIM), POOL_GDIM ** -0.5),
        "pool_scale": gain(ks[9], (DEPTH, POOL_WIDTH)),
        "sg_norm": gain(ks[10], (DEPTH, SG_WIDTH)),
        "sg_w": nrm(ks[11], (DEPTH, SG_GROUPS, CHUNK, CHUNK), CHUNK ** -0.5),
        "sg_b": gain(ks[12], (DEPTH, SG_GROUPS, CHUNK)),
        "attn_sink": nrm(ks[13], (DEPTH, N_Q_HEADS), 1.0),
        "w_branch": nrm(ks[14], (DEPTH, N_BRANCH, BRANCH_WIDTH, D_MODEL), BRANCH_WIDTH ** -0.5),
        "w_gate": nrm(ks[15], (DEPTH, N_BRANCH, D_MODEL, D_MODEL), D_MODEL ** -0.5),
        "b_gate": nrm(ks[16], (DEPTH, N_BRANCH, D_MODEL), 0.1),
        "w_out": nrm(ks[17], (DEPTH, D_MODEL, D_MODEL), D_MODEL ** -0.5),
        "w_ffn_in": nrm(ks[18], (DEPTH, D_MODEL, 2 * D_FF), D_MODEL ** -0.5),
        "w_ffn_out": nrm(ks[19], (DEPTH, D_FF, D_MODEL), D_FF ** -0.5),
    }


def reference(x_prompt, x_sample, norm_mix_pre, norm_mix_post, norm_ffn_pre, norm_ffn_post,
              w_in, conv_w, pool_w, pool_scale, sg_norm, sg_w, sg_b, attn_sink, w_branch,
              w_gate, b_gate, w_out, w_ffn_in, w_ffn_out):
    y_prompt = trunk(x_prompt, norm_mix_pre, norm_mix_post, norm_ffn_pre, norm_ffn_post, w_in, conv_w,
                     pool_w, pool_scale, sg_norm, sg_w, sg_b, attn_sink, w_branch, w_gate, b_gate,
                     w_out, w_ffn_in, w_ffn_out)
    y_sample = trunk(x_sample, norm_mix_pre, norm_mix_post, norm_ffn_pre, norm_ffn_post, w_in, conv_w,
                     pool_w, pool_scale, sg_norm, sg_w, sg_b, attn_sink, w_branch, w_gate, b_gate,
                     w_out, w_ffn_in, w_ffn_out)
    return (y_prompt, y_sample)
```

```python
import functools

import jax
import jax.numpy as jnp
from jax import lax
from jax.experimental import pallas as pl
from jax.experimental.pallas import tpu as pltpu

D_MODEL = 1024
EPS = 1e-6
BRANCH_WIDTH = 256
N_BRANCH = 4
POOL_WINDOWS = (2, 4, 8, 16)
POOL_GDIM = 64
CHUNK = 128
N_Q_HEADS = 4
N_KV_HEADS = 2
HEAD_DIM = 64
WINDOW = 128
ATTN_BLOCK = 128
NEG_BIG = -1e30
D_IN = 2048
D_FF = 2816

COL_BGATE, COL_CGATE, COL_XIN, COL_POOL = 0, 256, 512, 768
COL_SG_U, COL_SG_V, COL_Q, COL_K, COL_V = 1024, 1280, 1536, 1792, 1920
KV_COL_BLOCK = COL_K // 256

LANES = 128
HALO_ROWS = 16
N_COL_CHUNKS = D_MODEL // 256
N_FF_CHUNKS = D_FF // 256
VMEM_LIMIT_BYTES = 56 * 1024 * 1024

BF16 = jnp.bfloat16
F32 = jnp.float32


def _dot(a, b):
    return jnp.dot(a, b, preferred_element_type=F32)


def _rms_norm(x, gain):
    ms = jnp.mean(x * x, axis=-1, keepdims=True)
    return x * lax.rsqrt(ms + EPS) * gain


def _const_spec(shape):
    nd = len(shape)
    return pl.BlockSpec(shape, lambda i: (0,) * nd, pipeline_mode=pl.Buffered(1))


def _in_proj_kernel(x_ref, gain_ref, w_ref, z_ref):
    h = _rms_norm(x_ref[...], gain_ref[...])
    z_ref[...] = _dot(h.astype(BF16), w_ref[...]).astype(BF16)


def _in_proj(x, gain, w_in, tile):
    n = x.shape[0]
    return pl.pallas_call(
        _in_proj_kernel,
        grid=(n // tile,),
        in_specs=[
            pl.BlockSpec((tile, D_MODEL), lambda i: (i, 0)),
            _const_spec((1, D_MODEL)),
            _const_spec((D_MODEL, D_IN)),
        ],
        out_specs=pl.BlockSpec((tile, D_IN), lambda i: (i, 0)),
        out_shape=jax.ShapeDtypeStruct((n, D_IN), BF16),
        compiler_params=pltpu.CompilerParams(
            dimension_semantics=("arbitrary",), vmem_limit_bytes=VMEM_LIMIT_BYTES),
        name="in_proj",
    )(x, gain, w_in)


def _attn_bias(kh):
    row = lax.broadcasted_iota(jnp.int32, (2 * ATTN_BLOCK, 3 * ATTN_BLOCK), 0)
    col = lax.broadcasted_iota(jnp.int32, (2 * ATTN_BLOCK, 3 * ATTN_BLOCK), 1)
    qi = jnp.where(row >= ATTN_BLOCK, row - ATTN_BLOCK, row)
    rel = jnp.abs(qi - col + ATTN_BLOCK)
    slope_lo = 2.0 ** (-8.0 * (2 * kh + 1) / N_Q_HEADS)
    slope_hi = 2.0 ** (-8.0 * (2 * kh + 2) / N_Q_HEADS)
    slope = jnp.where(row >= ATTN_BLOCK, slope_hi, slope_lo)
    return jnp.where(rel <= WINDOW, -slope * rel.astype(F32), NEG_BIG)


def _mix_kernel(x_ref, z_ref, zprev_ref, znext_ref, kvprev_ref, kvnext_ref,
                gpre_ref, convw_ref, poolw_ref, poolscale_ref, sgnorm_ref, sgw_ref, sgb_ref,
                sink_ref, wg_ref, bg_ref, wb_ref, wo_ref, gpost_ref,
                o_ref,
                uext_ref, pext_ref, kvext_ref, br_ref, bias_ref,
                *, tile, seq_len):
    i = pl.program_id(0)
    tiles_per_seq = seq_len // tile
    t_in_seq = lax.rem(i, tiles_per_seq)
    is_first = t_in_seq == 0
    is_last = t_in_seq == tiles_per_seq - 1
    n_blocks = tile // ATTN_BLOCK

    @pl.when(i == 0)
    def _():
        for kh in range(N_KV_HEADS):
            bias_ref[kh] = _attn_bias(kh)

    x = x_ref[...]
    h = _rms_norm(x, gpre_ref[...]).astype(BF16)

    def gated_u(ref, rows):
        c = ref[rows, COL_CGATE:COL_CGATE + 256].astype(F32)
        xin = ref[rows, COL_XIN:COL_XIN + 256].astype(F32)
        return c * xin

    uext_ref[0:HALO_ROWS, :] = jnp.where(is_first, 0.0, gated_u(zprev_ref, slice(None)))
    uext_ref[HALO_ROWS:HALO_ROWS + tile, :] = gated_u(z_ref, slice(None))
    uext_ref[HALO_ROWS + tile:, :] = jnp.where(is_last, 0.0, gated_u(znext_ref, slice(None)))
    conv = (convw_ref[0:1, :] * uext_ref[HALO_ROWS - 1:HALO_ROWS - 1 + tile, :]
            + convw_ref[1:2, :] * uext_ref[HALO_ROWS:HALO_ROWS + tile, :]
            + convw_ref[2:3, :] * uext_ref[HALO_ROWS + 1:HALO_ROWS + 1 + tile, :])
    br_ref[0] = (z_ref[:, COL_BGATE:COL_BGATE + 256].astype(F32) * conv).astype(BF16)

    zp = z_ref[:, COL_POOL:COL_POOL + 256].astype(F32)
    pext_ref[0:HALO_ROWS, :] = jnp.where(
        is_first, 0.0, zprev_ref[:, COL_POOL:COL_POOL + 256].astype(F32))
    pext_ref[HALO_ROWS:HALO_ROWS + tile, :] = zp
    pext_ref[HALO_ROWS + tile:, :] = jnp.where(
        is_last, 0.0, znext_ref[:, COL_POOL:COL_POOL + 256].astype(F32))

    def shifted(off, lane_block):
        return pext_ref[HALO_ROWS + off:HALO_ROWS + off + tile,
                        lane_block * LANES:(lane_block + 1) * LANES]

    def window_sum(lo, hi, lane_block):
        acc = shifted(lo, lane_block)
        for off in range(lo + 1, hi):
            acc = acc + shifted(off, lane_block)
        return acc

    pos = t_in_seq * tile + lax.broadcasted_iota(jnp.int32, (tile, LANES), 0)
    lane = lax.broadcasted_iota(jnp.int32, (tile, LANES), 1)

    def count(w):
        hi = jnp.minimum(pos + w // 2, seq_len)
        lo = jnp.maximum(pos - w // 2, 0)
        return (hi - lo).astype(F32)

    s2 = window_sum(-1, 1, 0)
    s4 = s2 + shifted(-2, 0) + shifted(1, 0)
    s8 = window_sum(-4, 4, 1)
    s16 = s8 + window_sum(-8, -4, 1) + window_sum(4, 8, 1)
    pooled = jnp.concatenate(
        [jnp.where(lane < POOL_GDIM, s2 / count(2), s4 / count(4)),
         jnp.where(lane < POOL_GDIM, s8 / count(8), s16 / count(16))], axis=1) - zp
    br_ref[1] = (_dot(pooled.astype(BF16), poolw_ref[...]) * poolscale_ref[...]).astype(BF16)

    v = z_ref[:, COL_SG_V:COL_SG_V + 256].astype(F32)
    mu = jnp.mean(v, axis=-1, keepdims=True)
    vc = v - mu
    var = jnp.mean(vc * vc, axis=-1, keepdims=True)
    vln = (vc * lax.rsqrt(var + EPS) * sgnorm_ref[...]).astype(BF16)
    lane_b = lax.broadcasted_iota(jnp.int32, (CHUNK, LANES), 1)
    for n in range(tile // CHUNK):
        rows = slice(n * CHUNK, (n + 1) * CHUNK)
        mixed = []
        for j in range(2):
            vt = vln[rows, j * LANES:(j + 1) * LANES]
            rhs = jnp.concatenate(
                [jnp.where(lane_b < 64, vt, jnp.zeros_like(vt)),
                 jnp.where(lane_b >= 64, vt, jnp.zeros_like(vt))], axis=0)
            mixed.append(_dot(sgw_ref[j], rhs))
        mixed = jnp.concatenate(mixed, axis=1) + sgb_ref[...]
        u = z_ref[rows, COL_SG_U:COL_SG_U + 256].astype(F32)
        br_ref[2, rows, :] = (u * mixed).astype(BF16)

    kvext_ref[0:ATTN_BLOCK, :] = kvprev_ref[...]
    kvext_ref[ATTN_BLOCK:ATTN_BLOCK + tile, :] = z_ref[:, COL_K:COL_K + 256]
    kvext_ref[ATTN_BLOCK + tile:, :] = kvnext_ref[...]
    col = lax.broadcasted_iota(jnp.int32, (1, 3 * ATTN_BLOCK), 1)
    row2 = lax.broadcasted_iota(jnp.int32, (2 * ATTN_BLOCK, 1), 0)

    def attn_block(b, carry):
        r0 = pl.multiple_of(b * ATTN_BLOCK, ATTN_BLOCK)
        qb = z_ref[pl.ds(r0, ATTN_BLOCK), COL_Q:COL_Q + 256]
        kvb = kvext_ref[pl.ds(r0, 3 * ATTN_BLOCK), :]
        pen_prev = jnp.where(jnp.logical_and(is_first, b == 0), NEG_BIG, 0.0)
        pen_next = jnp.where(jnp.logical_and(is_last, b == n_blocks - 1), NEG_BIG, 0.0)
        edge = jnp.where(col < ATTN_BLOCK, pen_prev,
                         jnp.where(col >= 2 * ATTN_BLOCK, pen_next, 0.0))
        outs = []
        for kh in range(N_KV_HEADS):
            k = kvb[:, kh * HEAD_DIM:(kh + 1) * HEAD_DIM]
            vv = kvb[:, LANES + kh * HEAD_DIM:LANES + (kh + 1) * HEAD_DIM]
            q2 = jnp.concatenate(
                [qb[:, (2 * kh) * HEAD_DIM:(2 * kh + 1) * HEAD_DIM],
                 qb[:, (2 * kh + 1) * HEAD_DIM:(2 * kh + 2) * HEAD_DIM]], axis=0)
            s = lax.dot_general(q2, k, (((1,), (1,)), ((), ())), preferred_element_type=F32)
            s = s * (HEAD_DIM ** -0.5) + bias_ref[kh] + edge
            sink = jnp.where(row2 < ATTN_BLOCK, sink_ref[2 * kh], sink_ref[2 * kh + 1])
            m = jnp.maximum(jnp.max(s, axis=-1, keepdims=True), sink)
            p = jnp.exp(s - m)
            denom = jnp.sum(p, axis=-1, keepdims=True) + jnp.exp(sink - m)
            o = _dot(p.astype(BF16), vv) / denom
            outs += [o[:ATTN_BLOCK], o[ATTN_BLOCK:]]
        br_ref[3, pl.ds(r0, ATTN_BLOCK), :] = jnp.concatenate(outs, axis=1).astype(BF16)
        return carry

    lax.fori_loop(0, n_blocks, attn_block, 0)

    y = jnp.zeros((tile, D_MODEL), F32)
    for c in range(N_COL_CHUNKS):
        merged = jnp.zeros((tile, 256), F32)
        for b in range(N_BRANCH):
            idx = b * N_COL_CHUNKS + c
            gate = jax.nn.sigmoid(_dot(h, wg_ref[idx]) + bg_ref[idx])
            merged = merged + gate * _dot(br_ref[b], wb_ref[idx])
        y = y + _dot(merged.astype(BF16), wo_ref[c])
    o_ref[...] = x + _rms_norm(y, gpost_ref[...])


def _mix(x, z, p, tile, seq_len):
    n = x.shape[0]
    r16 = tile // HALO_ROWS
    r128 = tile // ATTN_BLOCK
    last16 = n // HALO_ROWS - 1
    last128 = n // ATTN_BLOCK - 1
    kernel = functools.partial(_mix_kernel, tile=tile, seq_len=seq_len)
    return pl.pallas_call(
        kernel,
        grid=(n // tile,),
        in_specs=[
            pl.BlockSpec((tile, D_MODEL), lambda i: (i, 0)),
            pl.BlockSpec((tile, D_IN), lambda i: (i, 0)),
            pl.BlockSpec((HALO_ROWS, 1024), lambda i: (jnp.maximum(i * r16 - 1, 0), 0)),
            pl.BlockSpec((HALO_ROWS, 1024), lambda i: (jnp.minimum((i + 1) * r16, last16), 0)),
            pl.BlockSpec((ATTN_BLOCK, 256),
                         lambda i: (jnp.maximum(i * r128 - 1, 0), KV_COL_BLOCK)),
            pl.BlockSpec((ATTN_BLOCK, 256),
                         lambda i: (jnp.minimum((i + 1) * r128, last128), KV_COL_BLOCK)),
            _const_spec((1, D_MODEL)),
            _const_spec((3, 256)),
            _const_spec((256, 256)),
            _const_spec((1, 256)),
            _const_spec((1, 256)),
            _const_spec((2, CHUNK, 2 * CHUNK)),
            _const_spec((CHUNK, 256)),
            pl.BlockSpec(memory_space=pltpu.SMEM),
            _const_spec((N_BRANCH * N_COL_CHUNKS, D_MODEL, 256)),
            _const_spec((N_BRANCH * N_COL_CHUNKS, 1, 256)),
            _const_spec((N_BRANCH * N_COL_CHUNKS, BRANCH_WIDTH, 256)),
            _const_spec((N_COL_CHUNKS, 256, D_MODEL)),
            _const_spec((1, D_MODEL)),
        ],
        out_specs=pl.BlockSpec((tile, D_MODEL), lambda i: (i, 0)),
        out_shape=jax.ShapeDtypeStruct((n, D_MODEL), F32),
        scratch_shapes=[
            pltpu.VMEM((tile + 2 * HALO_ROWS, 256), F32),
            pltpu.VMEM((tile + 2 * HALO_ROWS, 256), F32),
            pltpu.VMEM((tile + 2 * ATTN_BLOCK, 256), BF16),
            pltpu.VMEM((N_BRANCH, tile, BRANCH_WIDTH), BF16),
            pltpu.VMEM((N_KV_HEADS, 2 * ATTN_BLOCK, 3 * ATTN_BLOCK), F32),
        ],
        compiler_params=pltpu.CompilerParams(
            dimension_semantics=("arbitrary",), vmem_limit_bytes=VMEM_LIMIT_BYTES),
        name="mix",
    )(x, z, z, z, z, z,
      p["gpre"], p["convw"], p["poolw"], p["poolscale"], p["sgnorm"], p["sgw"], p["sgb"],
      p["sink"], p["wg"], p["bg"], p["wb"], p["wo"], p["gpost"])


def _ffn_kernel(x_ref, gpre_ref, wa_ref, wg_ref, wo_ref, gpost_ref, o_ref, *, tile):
    x = x_ref[...]
    h = _rms_norm(x, gpre_ref[...]).astype(BF16)
    y = jnp.zeros((tile, D_MODEL), F32)
    for c in range(N_FF_CHUNKS):
        a = _dot(h, wa_ref[c])
        g = _dot(h, wg_ref[c])
        act = (a * jax.nn.sigmoid(a) * g).astype(BF16)
        y = y + _dot(act, wo_ref[c])
    o_ref[...] = x + _rms_norm(y, gpost_ref[...])


def _ffn(x, p, tile):
    n = x.shape[0]
    return pl.pallas_call(
        functools.partial(_ffn_kernel, tile=tile),
        grid=(n // tile,),
        in_specs=[
            pl.BlockSpec((tile, D_MODEL), lambda i: (i, 0)),
            _const_spec((1, D_MODEL)),
            _const_spec((N_FF_CHUNKS, D_MODEL, 256)),
            _const_spec((N_FF_CHUNKS, D_MODEL, 256)),
            _const_spec((N_FF_CHUNKS, 256, D_MODEL)),
            _const_spec((1, D_MODEL)),
        ],
        out_specs=pl.BlockSpec((tile, D_MODEL), lambda i: (i, 0)),
        out_shape=jax.ShapeDtypeStruct((n, D_MODEL), F32),
        compiler_params=pltpu.CompilerParams(
            dimension_semantics=("arbitrary",), vmem_limit_bytes=VMEM_LIMIT_BYTES),
        name="ffn",
    )(x, p["ffn_gpre"], p["ffn_wa"], p["ffn_wg"], p["ffn_wo"], p["ffn_gpost"])


def _col_chunks(w):
    *lead, k, n = w.shape
    w = w.reshape(*lead, k, n // 256, 256)
    return jnp.moveaxis(w, -2, -3)


def _layer_params(l, norm_mix_pre, norm_mix_post, norm_ffn_pre, norm_ffn_post, w_in, conv_w,
                  pool_w, pool_scale, sg_norm, sg_w, sg_b, attn_sink, w_branch, w_gate, b_gate,
                  w_out, w_ffn_in, w_ffn_out):
    poolw = jnp.zeros((256, 256), F32)
    for g in range(len(POOL_WINDOWS)):
        poolw = poolw.at[g * POOL_GDIM:(g + 1) * POOL_GDIM,
                         g * POOL_GDIM:(g + 1) * POOL_GDIM].set(pool_w[l, g])
    sgw = jnp.stack([jnp.concatenate([sg_w[l, 2 * j], sg_w[l, 2 * j + 1]], axis=1)
                     for j in range(2)])
    return dict(
        w_in=w_in[l].astype(BF16),
        gpre=norm_mix_pre[l].reshape(1, D_MODEL),
        gpost=norm_mix_post[l].reshape(1, D_MODEL),
        convw=conv_w[l],
        poolw=poolw.astype(BF16),
        poolscale=pool_scale[l].reshape(1, 256),
        sgnorm=sg_norm[l].reshape(1, 256),
        sgw=sgw.astype(BF16),
        sgb=jnp.repeat(sg_b[l].T, 64, axis=1),
        sink=attn_sink[l],
        wg=_col_chunks(w_gate[l]).reshape(N_BRANCH * N_COL_CHUNKS, D_MODEL, 256).astype(BF16),
        bg=b_gate[l].reshape(N_BRANCH * N_COL_CHUNKS, 1, 256),
        wb=_col_chunks(w_branch[l]).reshape(N_BRANCH * N_COL_CHUNKS, BRANCH_WIDTH, 256).astype(BF16),
        wo=w_out[l].reshape(N_COL_CHUNKS, 256, D_MODEL).astype(BF16),
        ffn_gpre=norm_ffn_pre[l].reshape(1, D_MODEL),
        ffn_gpost=norm_ffn_post[l].reshape(1, D_MODEL),
        ffn_wa=_col_chunks(w_ffn_in[l][:, :D_FF]).astype(BF16),
        ffn_wg=_col_chunks(w_ffn_in[l][:, D_FF:]).astype(BF16),
        ffn_wo=w_ffn_out[l].reshape(N_FF_CHUNKS, 256, D_MODEL).astype(BF16),
    )


def _trunk(x, layers, tile):
    b, s, d = x.shape
    x = x.reshape(b * s, d)
    for p in layers:
        z = _in_proj(x, p["gpre"], p["w_in"], tile)
        x = _mix(x, z, p, tile, s)
        x = _ffn(x, p, tile)
    return x.reshape(b, s, d)


def kernel(x_prompt, x_sample, norm_mix_pre, norm_mix_post, norm_ffn_pre, norm_ffn_post, w_in,
           conv_w, pool_w, pool_scale, sg_norm, sg_w, sg_b, attn_sink, w_branch, w_gate, b_gate,
           w_out, w_ffn_in, w_ffn_out, *, tile=512):
    depth = w_in.shape[0]
    layers = [
        _layer_params(l, norm_mix_pre, norm_mix_post, norm_ffn_pre, norm_ffn_post, w_in, conv_w,
                      pool_w, pool_scale, sg_norm, sg_w, sg_b, attn_sink, w_branch, w_gate,
                      b_gate, w_out, w_ffn_in, w_ffn_out)
        for l in range(depth)]
    return (_trunk(x_prompt, layers, tile), _trunk(x_sample, layers, tile))
```

```python
import functools

import jax
import jax.numpy as jnp
from jax import lax
from jax.experimental import pallas as pl
from jax.experimental.pallas import tpu as pltpu

D_MODEL = 1024
EPS = 1e-6
BRANCH_WIDTH = 256
N_BRANCH = 4
POOL_WINDOWS = (2, 4, 8, 16)
POOL_GDIM = 64
CHUNK = 128
N_Q_HEADS = 4
N_KV_HEADS = 2
HEAD_DIM = 64
WINDOW = 128
ATTN_BLOCK = 128
NEG_BIG = -1e30
D_IN = 2048
D_FF = 2816

COL_BGATE, COL_CGATE, COL_XIN, COL_POOL = 0, 256, 512, 768
COL_SG_U, COL_SG_V, COL_Q, COL_K, COL_V = 1024, 1280, 1536, 1792, 1920
KV_COL_BLOCK = COL_K // 256

LANES = 128
HALO_ROWS = 16
N_COL_CHUNKS = D_MODEL // 256
N_FF_CHUNKS = D_FF // 256
VMEM_LIMIT_BYTES = 56 * 1024 * 1024

BF16 = jnp.bfloat16
F32 = jnp.float32


def _dot(a, b):
    return jnp.dot(a, b, preferred_element_type=F32)


def _rms_norm(x, gain):
    ms = jnp.mean(x * x, axis=-1, keepdims=True)
    return x * lax.rsqrt(ms + EPS) * gain


def _const_spec(shape):
    nd = len(shape)
    return pl.BlockSpec(shape, lambda i: (0,) * nd, pipeline_mode=pl.Buffered(1))


def _in_proj_kernel(x_ref, gain_ref, w_ref, z_ref):
    h = _rms_norm(x_ref[...], gain_ref[...])
    z_ref[...] = _dot(h.astype(BF16), w_ref[...]).astype(BF16)


def _in_proj(x, gain, w_in, tile):
    n = x.shape[0]
    return pl.pallas_call(
        _in_proj_kernel,
        grid=(n // tile,),
        in_specs=[
            pl.BlockSpec((tile, D_MODEL), lambda i: (i, 0)),
            _const_spec((1, D_MODEL)),
            _const_spec((D_MODEL, D_IN)),
        ],
        out_specs=pl.BlockSpec((tile, D_IN), lambda i: (i, 0)),
        out_shape=jax.ShapeDtypeStruct((n, D_IN), BF16),
        compiler_params=pltpu.CompilerParams(
            dimension_semantics=("arbitrary",), vmem_limit_bytes=VMEM_LIMIT_BYTES),
        name="in_proj",
    )(x, gain, w_in)


def _attn_bias(kh):
    row = lax.broadcasted_iota(jnp.int32, (2 * ATTN_BLOCK, 3 * ATTN_BLOCK), 0)
    col = lax.broadcasted_iota(jnp.int32, (2 * ATTN_BLOCK, 3 * ATTN_BLOCK), 1)
    qi = jnp.where(row >= ATTN_BLOCK, row - ATTN_BLOCK, row)
    rel = jnp.abs(qi - col + ATTN_BLOCK)
    slope_lo = 2.0 ** (-8.0 * (2 * kh + 1) / N_Q_HEADS)
    slope_hi = 2.0 ** (-8.0 * (2 * kh + 2) / N_Q_HEADS)
    slope = jnp.where(row >= ATTN_BLOCK, slope_hi, slope_lo)
    return jnp.where(rel <= WINDOW, -slope * rel.astype(F32), NEG_BIG)


def _mix_kernel(x_ref, z_ref, zprev_ref, znext_ref, kvprev_ref, kvnext_ref,
                gpre_ref, convw_ref, poolw_ref, poolscale_ref, sgnorm_ref, sgw_ref, sgb_ref,
                sink_ref, wg_ref, bg_ref, wb_ref, wo_ref, gpost_ref,
                o_ref,
                uext_ref, pext_ref, kvext_ref, br_ref, bias_ref, merged_ref,
                *, tile, seq_len):
    i = pl.program_id(0)
    tiles_per_seq = seq_len // tile
    t_in_seq = lax.rem(i, tiles_per_seq)
    is_first = t_in_seq == 0
    is_last = t_in_seq == tiles_per_seq - 1
    n_blocks = tile // ATTN_BLOCK

    @pl.when(i == 0)
    def _():
        for kh in range(N_KV_HEADS):
            bias_ref[kh] = _attn_bias(kh)

    kvext_ref[0:ATTN_BLOCK, :] = kvprev_ref[...]
    kvext_ref[ATTN_BLOCK:ATTN_BLOCK + tile, :] = z_ref[:, COL_K:COL_K + 256]
    kvext_ref[ATTN_BLOCK + tile:, :] = kvnext_ref[...]
    col = lax.broadcasted_iota(jnp.int32, (1, 3 * ATTN_BLOCK), 1)
    row2 = lax.broadcasted_iota(jnp.int32, (2 * ATTN_BLOCK, 1), 0)

    def attn_scores(b):
        r0 = b * ATTN_BLOCK
        qb = z_ref[r0:r0 + ATTN_BLOCK, COL_Q:COL_Q + 256]
        edge = jnp.zeros((1, 3 * ATTN_BLOCK), F32)
        if b == 0:
            edge = jnp.where(jnp.logical_and(is_first, col < ATTN_BLOCK), NEG_BIG, edge)
        if b == n_blocks - 1:
            edge = jnp.where(jnp.logical_and(is_last, col >= 2 * ATTN_BLOCK), NEG_BIG, edge)
        scores = []
        for kh in range(N_KV_HEADS):
            k = kvext_ref[r0:r0 + 3 * ATTN_BLOCK, kh * HEAD_DIM:(kh + 1) * HEAD_DIM]
            q2 = jnp.concatenate(
                [qb[:, (2 * kh) * HEAD_DIM:(2 * kh + 1) * HEAD_DIM],
                 qb[:, (2 * kh + 1) * HEAD_DIM:(2 * kh + 2) * HEAD_DIM]], axis=0)
            s = lax.dot_general(q2, k, (((1,), (1,)), ((), ())), preferred_element_type=F32)
            s = s * (HEAD_DIM ** -0.5) + bias_ref[kh]
            if b == 0 or b == n_blocks - 1:
                s = s + edge
            scores.append(s)
        return scores

    def attn_values(b, scores):
        r0 = b * ATTN_BLOCK
        outs = []
        for kh in range(N_KV_HEADS):
            s = scores[kh]
            vv = kvext_ref[r0:r0 + 3 * ATTN_BLOCK,
                           LANES + kh * HEAD_DIM:LANES + (kh + 1) * HEAD_DIM]
            sink = jnp.where(row2 < ATTN_BLOCK, sink_ref[2 * kh], sink_ref[2 * kh + 1])
            m = jnp.maximum(jnp.max(s, axis=-1, keepdims=True), sink)
            p = jnp.exp(s - m)
            denom = jnp.sum(p, axis=-1, keepdims=True) + jnp.exp(sink - m)
            o = _dot(p.astype(BF16), vv) / denom
            outs += [o[:ATTN_BLOCK], o[ATTN_BLOCK:]]
        br_ref[3, r0:r0 + ATTN_BLOCK, :] = jnp.concatenate(outs, axis=1).astype(BF16)

    def conv_branch():
        def gated_u(ref):
            c = ref[:, COL_CGATE:COL_CGATE + 256].astype(F32)
            xin = ref[:, COL_XIN:COL_XIN + 256].astype(F32)
            return c * xin

        uext_ref[0:HALO_ROWS, :] = jnp.where(is_first, 0.0, gated_u(zprev_ref))
        uext_ref[HALO_ROWS:HALO_ROWS + tile, :] = gated_u(z_ref)
        uext_ref[HALO_ROWS + tile:, :] = jnp.where(is_last, 0.0, gated_u(znext_ref))
        conv = (convw_ref[0:1, :] * uext_ref[HALO_ROWS - 1:HALO_ROWS - 1 + tile, :]
                + convw_ref[1:2, :] * uext_ref[HALO_ROWS:HALO_ROWS + tile, :]
                + convw_ref[2:3, :] * uext_ref[HALO_ROWS + 1:HALO_ROWS + 1 + tile, :])
        br_ref[0] = (z_ref[:, COL_BGATE:COL_BGATE + 256].astype(F32) * conv).astype(BF16)

    def pool_branch():
        zp = z_ref[:, COL_POOL:COL_POOL + 256].astype(F32)
        pext_ref[0:HALO_ROWS, :] = jnp.where(
            is_first, 0.0, zprev_ref[:, COL_POOL:COL_POOL + 256].astype(F32))
        pext_ref[HALO_ROWS:HALO_ROWS + tile, :] = zp
        pext_ref[HALO_ROWS + tile:, :] = jnp.where(
            is_last, 0.0, znext_ref[:, COL_POOL:COL_POOL + 256].astype(F32))

        def shifted(off, lane_block):
            return pext_ref[HALO_ROWS + off:HALO_ROWS + off + tile,
                            lane_block * LANES:(lane_block + 1) * LANES]

        def window_sum(lo, hi, lane_block):
            acc = shifted(lo, lane_block)
            for off in range(lo + 1, hi):
                acc = acc + shifted(off, lane_block)
            return acc

        pos = t_in_seq * tile + lax.broadcasted_iota(jnp.int32, (tile, LANES), 0)
        lane = lax.broadcasted_iota(jnp.int32, (tile, LANES), 1)

        def count(w):
            hi = jnp.minimum(pos + w // 2, seq_len)
            lo = jnp.maximum(pos - w // 2, 0)
            return (hi - lo).astype(F32)

        s2 = window_sum(-1, 1, 0)
        s4 = s2 + shifted(-2, 0) + shifted(1, 0)
        s8 = window_sum(-4, 4, 1)
        s16 = s8 + window_sum(-8, -4, 1) + window_sum(4, 8, 1)
        pooled = jnp.concatenate(
            [jnp.where(lane < POOL_GDIM, s2 / count(2), s4 / count(4)),
             jnp.where(lane < POOL_GDIM, s8 / count(8), s16 / count(16))], axis=1) - zp
        br_ref[1] = (_dot(pooled.astype(BF16), poolw_ref[...]) * poolscale_ref[...]).astype(BF16)

    def gating_branch():
        v = z_ref[:, COL_SG_V:COL_SG_V + 256].astype(F32)
        mu = jnp.mean(v, axis=-1, keepdims=True)
        vc = v - mu
        var = jnp.mean(vc * vc, axis=-1, keepdims=True)
        vln = (vc * lax.rsqrt(var + EPS) * sgnorm_ref[...]).astype(BF16)
        lane_b = lax.broadcasted_iota(jnp.int32, (CHUNK, LANES), 1)
        for n in range(tile // CHUNK):
            rows = slice(n * CHUNK, (n + 1) * CHUNK)
            mixed = []
            for j in range(2):
                vt = vln[rows, j * LANES:(j + 1) * LANES]
                rhs = jnp.concatenate(
                    [jnp.where(lane_b < 64, vt, jnp.zeros_like(vt)),
                     jnp.where(lane_b >= 64, vt, jnp.zeros_like(vt))], axis=0)
                mixed.append(_dot(sgw_ref[j], rhs))
            mixed = jnp.concatenate(mixed, axis=1) + sgb_ref[...]
            u = z_ref[rows, COL_SG_U:COL_SG_U + 256].astype(F32)
            br_ref[2, rows, :] = (u * mixed).astype(BF16)

    def gates(b, h):
        return [jax.nn.sigmoid(_dot(h, wg_ref[b * N_COL_CHUNKS + c]) + bg_ref[b * N_COL_CHUNKS + c])
                for c in range(N_COL_CHUNKS)]

    def project(b, gate):
        for c in range(N_COL_CHUNKS):
            term = gate[c] * _dot(br_ref[b], wb_ref[b * N_COL_CHUNKS + c])
            cols = slice(c * 256, (c + 1) * 256)
            if b == 0:
                merged_ref[:, cols] = term
            else:
                merged_ref[:, cols] += term

    branch_fns = [conv_branch, pool_branch, gating_branch, None]
    blocks_per_branch = n_blocks // N_BRANCH

    x = x_ref[...]
    h = _rms_norm(x, gpre_ref[...]).astype(BF16)
    for b in range(N_BRANCH):
        blocks = range(b * blocks_per_branch, (b + 1) * blocks_per_branch)
        scores = [attn_scores(blk) for blk in blocks]
        gate = gates(b, h)
        for blk, s in zip(blocks, scores):
            attn_values(blk, s)
        if branch_fns[b] is not None:
            branch_fns[b]()
        project(b, gate)

    y = _dot(merged_ref[...].astype(BF16), wo_ref[...])
    o_ref[...] = x + _rms_norm(y, gpost_ref[...])


def _mix(x, z, p, tile, seq_len):
    n = x.shape[0]
    assert tile % (N_BRANCH * ATTN_BLOCK) == 0 and seq_len % tile == 0
    r16 = tile // HALO_ROWS
    r128 = tile // ATTN_BLOCK
    last16 = n // HALO_ROWS - 1
    last128 = n // ATTN_BLOCK - 1
    kernel = functools.partial(_mix_kernel, tile=tile, seq_len=seq_len)
    return pl.pallas_call(
        kernel,
        grid=(n // tile,),
        in_specs=[
            pl.BlockSpec((tile, D_MODEL), lambda i: (i, 0)),
            pl.BlockSpec((tile, D_IN), lambda i: (i, 0)),
            pl.BlockSpec((HALO_ROWS, 1024), lambda i: (jnp.maximum(i * r16 - 1, 0), 0)),
            pl.BlockSpec((HALO_ROWS, 1024), lambda i: (jnp.minimum((i + 1) * r16, last16), 0)),
            pl.BlockSpec((ATTN_BLOCK, 256),
                         lambda i: (jnp.maximum(i * r128 - 1, 0), KV_COL_BLOCK)),
            pl.BlockSpec((ATTN_BLOCK, 256),
                         lambda i: (jnp.minimum((i + 1) * r128, last128), KV_COL_BLOCK)),
            _const_spec((1, D_MODEL)),
            _const_spec((3, 256)),
            _const_spec((256, 256)),
            _const_spec((1, 256)),
            _const_spec((1, 256)),
            _const_spec((2, CHUNK, 2 * CHUNK)),
            _const_spec((CHUNK, 256)),
            pl.BlockSpec(memory_space=pltpu.SMEM),
            _const_spec((N_BRANCH * N_COL_CHUNKS, D_MODEL, 256)),
            _const_spec((N_BRANCH * N_COL_CHUNKS, 1, 256)),
            _const_spec((N_BRANCH * N_COL_CHUNKS, BRANCH_WIDTH, 256)),
            _const_spec((D_MODEL, D_MODEL)),
            _const_spec((1, D_MODEL)),
        ],
        out_specs=pl.BlockSpec((tile, D_MODEL), lambda i: (i, 0)),
        out_shape=jax.ShapeDtypeStruct((n, D_MODEL), F32),
        scratch_shapes=[
            pltpu.VMEM((tile + 2 * HALO_ROWS, 256), F32),
            pltpu.VMEM((tile + 2 * HALO_ROWS, 256), F32),
            pltpu.VMEM((tile + 2 * ATTN_BLOCK, 256), BF16),
            pltpu.VMEM((N_BRANCH, tile, BRANCH_WIDTH), BF16),
            pltpu.VMEM((N_KV_HEADS, 2 * ATTN_BLOCK, 3 * ATTN_BLOCK), F32),
            pltpu.VMEM((tile, D_MODEL), F32),
        ],
        compiler_params=pltpu.CompilerParams(
            dimension_semantics=("arbitrary",), vmem_limit_bytes=VMEM_LIMIT_BYTES),
        name="mix",
    )(x, z, z, z, z, z,
      p["gpre"], p["convw"], p["poolw"], p["poolscale"], p["sgnorm"], p["sgw"], p["sgb"],
      p["sink"], p["wg"], p["bg"], p["wb"], p["wo"], p["gpost"])


def _ffn_kernel(x_ref, gpre_ref, wa_ref, wg_ref, wo_ref, gpost_ref, o_ref, *, tile):
    x = x_ref[...]
    h = _rms_norm(x, gpre_ref[...]).astype(BF16)
    y = jnp.zeros((tile, D_MODEL), F32)
    for c in range(N_FF_CHUNKS):
        a = _dot(h, wa_ref[c])
        g = _dot(h, wg_ref[c])
        act = (a * jax.nn.sigmoid(a) * g).astype(BF16)
        y = y + _dot(act, wo_ref[c])
    o_ref[...] = x + _rms_norm(y, gpost_ref[...])


def _ffn(x, p, tile):
    n = x.shape[0]
    return pl.pallas_call(
        functools.partial(_ffn_kernel, tile=tile),
        grid=(n // tile,),
        in_specs=[
            pl.BlockSpec((tile, D_MODEL), lambda i: (i, 0)),
            _const_spec((1, D_MODEL)),
            _const_spec((N_FF_CHUNKS, D_MODEL, 256)),
            _const_spec((N_FF_CHUNKS, D_MODEL, 256)),
            _const_spec((N_FF_CHUNKS, 256, D_MODEL)),
            _const_spec((1, D_MODEL)),
        ],
        out_specs=pl.BlockSpec((tile, D_MODEL), lambda i: (i, 0)),
        out_shape=jax.ShapeDtypeStruct((n, D_MODEL), F32),
        compiler_params=pltpu.CompilerParams(
            dimension_semantics=("arbitrary",), vmem_limit_bytes=VMEM_LIMIT_BYTES),
        name="ffn",
    )(x, p["ffn_gpre"], p["ffn_wa"], p["ffn_wg"], p["ffn_wo"], p["ffn_gpost"])


def _col_chunks(w):
    *lead, k, n = w.shape
    w = w.reshape(*lead, k, n // 256, 256)
    return jnp.moveaxis(w, -2, -3)


def _layer_params(l, norm_mix_pre, norm_mix_post, norm_ffn_pre, norm_ffn_post, w_in, conv_w,
                  pool_w, pool_scale, sg_norm, sg_w, sg_b, attn_sink, w_branch, w_gate, b_gate,
                  w_out, w_ffn_in, w_ffn_out):
    poolw = jnp.zeros((256, 256), F32)
    for g in range(len(POOL_WINDOWS)):
        poolw = poolw.at[g * POOL_GDIM:(g + 1) * POOL_GDIM,
                         g * POOL_GDIM:(g + 1) * POOL_GDIM].set(pool_w[l, g])
    sgw = jnp.stack([jnp.concatenate([sg_w[l, 2 * j], sg_w[l, 2 * j + 1]], axis=1)
                     for j in range(2)])
    return dict(
        w_in=w_in[l].astype(BF16),
        gpre=norm_mix_pre[l].reshape(1, D_MODEL),
        gpost=norm_mix_post[l].reshape(1, D_MODEL),
        convw=conv_w[l],
        poolw=poolw.astype(BF16),
        poolscale=pool_scale[l].reshape(1, 256),
        sgnorm=sg_norm[l].reshape(1, 256),
        sgw=sgw.astype(BF16),
        sgb=jnp.repeat(sg_b[l].T, 64, axis=1),
        sink=attn_sink[l],
        wg=_col_chunks(w_gate[l]).reshape(N_BRANCH * N_COL_CHUNKS, D_MODEL, 256).astype(BF16),
        bg=b_gate[l].reshape(N_BRANCH * N_COL_CHUNKS, 1, 256),
        wb=_col_chunks(w_branch[l]).reshape(N_BRANCH * N_COL_CHUNKS, BRANCH_WIDTH, 256).astype(BF16),
        wo=w_out[l].astype(BF16),
        ffn_gpre=norm_ffn_pre[l].reshape(1, D_MODEL),
        ffn_gpost=norm_ffn_post[l].reshape(1, D_MODEL),
        ffn_wa=_col_chunks(w_ffn_in[l][:, :D_FF]).astype(BF16),
        ffn_wg=_col_chunks(w_ffn_in[l][:, D_FF:]).astype(BF16),
        ffn_wo=w_ffn_out[l].reshape(N_FF_CHUNKS, 256, D_MODEL).astype(BF16),
    )


def _trunk(x, layers, tile):
    b, s, d = x.shape
    x = x.reshape(b * s, d)
    for p in layers:
        z = _in_proj(x, p["gpre"], p["w_in"], tile)
        x = _mix(x, z, p, tile, s)
        x = _ffn(x, p, tile)
    return x.reshape(b, s, d)


def kernel(x_prompt, x_sample, norm_mix_pre, norm_mix_post, norm_ffn_pre, norm_ffn_post, w_in,
           conv_w, pool_w, pool_scale, sg_norm, sg_w, sg_b, attn_sink, w_branch, w_gate, b_gate,
           w_out, w_ffn_in, w_ffn_out, *, tile=512):
    depth = w_in.shape[0]
    layers = [
        _layer_params(l, norm_mix_pre, norm_mix_post, norm_ffn_pre, norm_ffn_post, w_in, conv_w,
                      pool_w, pool_scale, sg_norm, sg_w, sg_b, attn_sink, w_branch, w_gate,
                      b_gate, w_out, w_ffn_in, w_ffn_out)
        for l in range(depth)]
    return (_trunk(x_prompt, layers, tile), _trunk(x_sample, layers, tile))
```

```python
import functools

import jax
import jax.numpy as jnp
from jax import lax
from jax.experimental import pallas as pl
from jax.experimental.pallas import tpu as pltpu

D_MODEL = 1024
EPS = 1e-6
BRANCH_WIDTH = 256
N_BRANCH = 4
POOL_GDIM = 64
CHUNK = 128
SG_GROUPS = 4
SG_GDIM = 64
N_Q_HEADS = 4
N_KV_HEADS = 2
HEAD_DIM = 64
WINDOW = 128
ATTN_BLOCK = 128
NEG_BIG = -1e30
D_IN = 2048
D_FF = 2816

COL_BGATE, COL_CGATE, COL_XIN, COL_POOL = 0, 256, 512, 768
COL_SG_U, COL_SG_V, COL_Q, COL_K, COL_V = 1024, 1280, 1536, 1792, 1920
KV_COL_BLOCK = COL_K // 256

LANES = 128
MXU = 256
HALO_ROWS = 16
N_COL_CHUNKS = D_MODEL // MXU
N_FF_CHUNKS = D_FF // MXU
TILE = 512
FFN_TILE = 1024
VMEM_LIMIT_BYTES = 56 * 1024 * 1024

BF16 = jnp.bfloat16
F32 = jnp.float32


def _dot(a, b):
    return jnp.dot(a, b, preferred_element_type=F32)


def _rms_norm(x, gain):
    ms = jnp.mean(x * x, axis=-1, keepdims=True)
    return x * lax.rsqrt(ms + EPS) * gain


def _const_spec(shape):
    nd = len(shape)
    return pl.BlockSpec(shape, lambda i: (0,) * nd, pipeline_mode=pl.Buffered(1))


def _in_proj_kernel(x_ref, gain_ref, w_ref, z_ref):
    h = _rms_norm(x_ref[...], gain_ref[...])
    z_ref[...] = _dot(h.astype(BF16), w_ref[...]).astype(BF16)


def _in_proj(x, gain, w_in):
    n = x.shape[0]
    return pl.pallas_call(
        _in_proj_kernel,
        grid=(n // TILE,),
        in_specs=[
            pl.BlockSpec((TILE, D_MODEL), lambda i: (i, 0)),
            _const_spec((1, D_MODEL)),
            _const_spec((D_MODEL, D_IN)),
        ],
        out_specs=pl.BlockSpec((TILE, D_IN), lambda i: (i, 0)),
        out_shape=jax.ShapeDtypeStruct((n, D_IN), BF16),
        compiler_params=pltpu.CompilerParams(
            dimension_semantics=("arbitrary",), vmem_limit_bytes=VMEM_LIMIT_BYTES),
        name="in_proj",
    )(x, gain, w_in)


def _attn_bias(kh):
    row = lax.broadcasted_iota(jnp.int32, (2 * ATTN_BLOCK, 3 * ATTN_BLOCK), 0)
    col = lax.broadcasted_iota(jnp.int32, (2 * ATTN_BLOCK, 3 * ATTN_BLOCK), 1)
    qi = jnp.where(row >= ATTN_BLOCK, row - ATTN_BLOCK, row)
    rel = jnp.abs(qi - col + ATTN_BLOCK)
    slope_lo = 2.0 ** (-8.0 * (2 * kh + 1) / N_Q_HEADS)
    slope_hi = 2.0 ** (-8.0 * (2 * kh + 2) / N_Q_HEADS)
    slope = jnp.where(row >= ATTN_BLOCK, slope_hi, slope_lo)
    return jnp.where(rel <= WINDOW, -slope * rel.astype(F32), NEG_BIG)


def _mix_kernel(x_ref, z_ref, zprev_ref, znext_ref, kvprev_ref, kvnext_ref,
                gpre_ref, convw_ref, poolw_ref, poolscale_ref, sgnorm_ref, sgwt_ref, sgb_ref,
                sink_ref, wg_ref, bg_ref, wb_ref, wo_ref, gpost_ref,
                o_ref,
                uext_ref, pext_ref, psum_ref, kvext_ref, kt_ref, br_ref, bias_ref, merged_ref,
                *, seq_len):
    tile = TILE
    i = pl.program_id(0)
    tiles_per_seq = seq_len // tile
    t_in_seq = lax.rem(i, tiles_per_seq)
    is_first = t_in_seq == 0
    is_last = t_in_seq == tiles_per_seq - 1
    n_blocks = tile // ATTN_BLOCK

    @pl.when(i == 0)
    def _():
        for kh in range(N_KV_HEADS):
            bias_ref[kh] = _attn_bias(kh)

    def attn_prepare():
        kvext_ref[0:ATTN_BLOCK, :] = kvprev_ref[...]
        kvext_ref[ATTN_BLOCK:ATTN_BLOCK + tile, :] = z_ref[:, COL_K:COL_K + 256]
        kvext_ref[ATTN_BLOCK + tile:, :] = kvnext_ref[...]
        kt_ref[...] = kvext_ref[:, 0:LANES].astype(F32).T.astype(BF16)

    col = lax.broadcasted_iota(jnp.int32, (1, 3 * ATTN_BLOCK), 1)
    row2 = lax.broadcasted_iota(jnp.int32, (2 * ATTN_BLOCK, 1), 0)

    def attn_scores(b):
        r0 = b * ATTN_BLOCK
        qb = z_ref[r0:r0 + ATTN_BLOCK, COL_Q:COL_Q + 256]
        edge = jnp.zeros((1, 3 * ATTN_BLOCK), F32)
        if b == 0:
            edge = jnp.where(jnp.logical_and(is_first, col < ATTN_BLOCK), NEG_BIG, edge)
        if b == n_blocks - 1:
            edge = jnp.where(jnp.logical_and(is_last, col >= 2 * ATTN_BLOCK), NEG_BIG, edge)
        scores = []
        for kh in range(N_KV_HEADS):
            kt = kt_ref[kh * HEAD_DIM:(kh + 1) * HEAD_DIM, r0:r0 + 3 * ATTN_BLOCK]
            q2 = jnp.concatenate(
                [qb[:, (2 * kh) * HEAD_DIM:(2 * kh + 1) * HEAD_DIM],
                 qb[:, (2 * kh + 1) * HEAD_DIM:(2 * kh + 2) * HEAD_DIM]], axis=0)
            s = _dot(q2, kt) * (HEAD_DIM ** -0.5) + bias_ref[kh]
            if b == 0 or b == n_blocks - 1:
                s = s + edge
            scores.append(s)
        return scores

    def attn_values(b, scores):
        r0 = b * ATTN_BLOCK
        outs = []
        for kh in range(N_KV_HEADS):
            s = scores[kh]
            vv = kvext_ref[r0:r0 + 3 * ATTN_BLOCK,
                           LANES + kh * HEAD_DIM:LANES + (kh + 1) * HEAD_DIM]
            sink = jnp.where(row2 < ATTN_BLOCK, sink_ref[2 * kh], sink_ref[2 * kh + 1])
            m = jnp.maximum(jnp.max(s, axis=-1, keepdims=True), sink)
            p = jnp.exp(s - m)
            denom = jnp.sum(p, axis=-1, keepdims=True) + jnp.exp(sink - m)
            o = _dot(p.astype(BF16), vv) / denom
            outs += [o[:ATTN_BLOCK], o[ATTN_BLOCK:]]
        br_ref[3, r0:r0 + ATTN_BLOCK, :] = jnp.concatenate(outs, axis=1).astype(BF16)

    def conv_branch():
        def gated_u(ref):
            c = ref[:, COL_CGATE:COL_CGATE + 256].astype(F32)
            xin = ref[:, COL_XIN:COL_XIN + 256].astype(F32)
            return c * xin

        uext_ref[0:HALO_ROWS, :] = jnp.where(is_first, 0.0, gated_u(zprev_ref))
        uext_ref[HALO_ROWS:HALO_ROWS + tile, :] = gated_u(z_ref)
        uext_ref[HALO_ROWS + tile:, :] = jnp.where(is_last, 0.0, gated_u(znext_ref))
        conv = (convw_ref[0:1, :] * uext_ref[HALO_ROWS - 1:HALO_ROWS - 1 + tile, :]
                + convw_ref[1:2, :] * uext_ref[HALO_ROWS:HALO_ROWS + tile, :]
                + convw_ref[2:3, :] * uext_ref[HALO_ROWS + 1:HALO_ROWS + 1 + tile, :])
        br_ref[0] = (z_ref[:, COL_BGATE:COL_BGATE + 256].astype(F32) * conv).astype(BF16)

    def pool_fill():
        pext_ref[0:HALO_ROWS, :] = jnp.where(
            is_first, 0.0, zprev_ref[:, COL_POOL:COL_POOL + 256].astype(F32))
        pext_ref[HALO_ROWS:HALO_ROWS + tile, :] = z_ref[:, COL_POOL:COL_POOL + 256].astype(F32)
        pext_ref[HALO_ROWS + tile:, :] = jnp.where(
            is_last, 0.0, znext_ref[:, COL_POOL:COL_POOL + 256].astype(F32))

    def pool_sums_wide():
        rows = tile + HALO_ROWS
        acc = pext_ref[HALO_ROWS - 8:HALO_ROWS - 8 + rows, LANES:]
        for back in range(1, 8):
            acc = acc + pext_ref[HALO_ROWS - back:HALO_ROWS - back + rows, LANES:]
        psum_ref[...] = acc

    def inv_count(w, row0, rows):
        pos = t_in_seq * tile + row0 + lax.broadcasted_iota(jnp.int32, (rows, LANES), 0)
        hi = jnp.minimum(pos + w // 2, seq_len)
        lo = jnp.maximum(pos - w // 2, 0)
        return 1.0 / (hi - lo).astype(F32)

    def pool_branch():
        def shifted(off, rows):
            return pext_ref[HALO_ROWS + off + rows.start:HALO_ROWS + off + rows.stop, 0:LANES]

        def pooled_rows(rows, exact_counts):
            n = rows.stop - rows.start
            lane = lax.broadcasted_iota(jnp.int32, (n, LANES), 1)
            s2 = shifted(-1, rows) + shifted(0, rows)
            s4 = s2 + shifted(-2, rows) + shifted(1, rows)
            s8 = psum_ref[rows.start + 4:rows.stop + 4, :]
            s16 = psum_ref[rows.start:rows.stop, :] + psum_ref[rows.start + 8:rows.stop + 8, :]
            if exact_counts:
                inv = [inv_count(w, rows.start, n) for w in (2, 4, 8, 16)]
            else:
                inv = [1.0 / w for w in (2, 4, 8, 16)]
            mean = jnp.concatenate(
                [jnp.where(lane < POOL_GDIM, s2 * inv[0], s4 * inv[1]),
                 jnp.where(lane < POOL_GDIM, s8 * inv[2], s16 * inv[3])], axis=1)
            return mean - pext_ref[HALO_ROWS + rows.start:HALO_ROWS + rows.stop, :]

        pooled = jnp.concatenate(
            [pooled_rows(slice(0, HALO_ROWS), True),
             pooled_rows(slice(HALO_ROWS, tile - HALO_ROWS), False),
             pooled_rows(slice(tile - HALO_ROWS, tile), True)], axis=0)
        br_ref[1] = (_dot(pooled.astype(BF16), poolw_ref[...]) * poolscale_ref[...]).astype(BF16)

    def gating_branch():
        v = z_ref[:, COL_SG_V:COL_SG_V + 256].astype(F32)
        mu = jnp.mean(v, axis=-1, keepdims=True)
        vc = v - mu
        var = jnp.mean(vc * vc, axis=-1, keepdims=True)
        vln_t = (vc * lax.rsqrt(var + EPS) * sgnorm_ref[...]).T
        n_chunks = tile // CHUNK
        mixed_t = []
        for g in range(SG_GROUPS):
            lhs = jnp.concatenate(
                [vln_t[g * SG_GDIM:(g + 1) * SG_GDIM, n * CHUNK:(n + 1) * CHUNK]
                 for n in range(n_chunks)], axis=0)
            mixed_t.append(_dot(lhs.astype(BF16), sgwt_ref[g]))
        for n in range(n_chunks):
            rows = slice(n * CHUNK, (n + 1) * CHUNK)
            tiles = []
            for j in range(2):
                piece = jnp.concatenate(
                    [mixed_t[2 * j][n * SG_GDIM:(n + 1) * SG_GDIM],
                     mixed_t[2 * j + 1][n * SG_GDIM:(n + 1) * SG_GDIM]], axis=0)
                tiles.append(piece.T)
            mixed = jnp.concatenate(tiles, axis=1) + sgb_ref[...]
            u = z_ref[rows, COL_SG_U:COL_SG_U + 256].astype(F32)
            br_ref[2, rows, :] = (u * mixed).astype(BF16)

    def gate(b, c, h):
        cols = slice(c * MXU, (c + 1) * MXU)
        return jax.nn.sigmoid(_dot(h, wg_ref[b, :, cols]) + bg_ref[b:b + 1, cols])

    def project(b, gates):
        for c in range(N_COL_CHUNKS):
            cols = slice(c * MXU, (c + 1) * MXU)
            term = gates[c] * _dot(br_ref[b], wb_ref[b, :, cols])
            if b == 0:
                merged_ref[:, cols] = term
            else:
                merged_ref[:, cols] += term

    attn_prepare()
    sc0 = attn_scores(0)
    sc1 = attn_scores(1)
    x = x_ref[...]
    h = _rms_norm(x, gpre_ref[...]).astype(BF16)

    g = [gate(0, 0, h)]
    attn_values(0, sc0)
    g.append(gate(0, 1, h))
    conv_branch()
    g.append(gate(0, 2, h))
    g.append(gate(0, 3, h))
    project(0, g)

    g = [gate(2, 0, h)]
    gating_branch()
    g.append(gate(2, 1, h))
    attn_values(1, sc1)
    g.append(gate(2, 2, h))
    pool_fill()
    pool_sums_wide()
    g.append(gate(2, 3, h))
    project(2, g)

    g = [gate(1, 0, h)]
    pool_branch()
    g.append(gate(1, 1, h))
    sc2 = attn_scores(2)
    g.append(gate(1, 2, h))
    attn_values(2, sc2)
    g.append(gate(1, 3, h))
    project(1, g)

    g = [gate(3, 0, h)]
    sc3 = attn_scores(3)
    g.append(gate(3, 1, h))
    attn_values(3, sc3)
    g.append(gate(3, 2, h))
    g.append(gate(3, 3, h))
    project(3, g)

    y = _dot(merged_ref[...].astype(BF16), wo_ref[...])
    o_ref[...] = x + _rms_norm(y, gpost_ref[...])


def _mix(x, z, p, seq_len):
    n = x.shape[0]
    tile = TILE
    assert tile == N_BRANCH * ATTN_BLOCK and seq_len % tile == 0
    r16 = tile // HALO_ROWS
    r128 = tile // ATTN_BLOCK
    last16 = n // HALO_ROWS - 1
    last128 = n // ATTN_BLOCK - 1
    kernel = functools.partial(_mix_kernel, seq_len=seq_len)
    return pl.pallas_call(
        kernel,
        grid=(n // tile,),
        in_specs=[
            pl.BlockSpec((tile, D_MODEL), lambda i: (i, 0)),
            pl.BlockSpec((tile, D_IN), lambda i: (i, 0)),
            pl.BlockSpec((HALO_ROWS, 1024), lambda i: (jnp.maximum(i * r16 - 1, 0), 0)),
            pl.BlockSpec((HALO_ROWS, 1024), lambda i: (jnp.minimum((i + 1) * r16, last16), 0)),
            pl.BlockSpec((ATTN_BLOCK, 256),
                         lambda i: (jnp.maximum(i * r128 - 1, 0), KV_COL_BLOCK)),
            pl.BlockSpec((ATTN_BLOCK, 256),
                         lambda i: (jnp.minimum((i + 1) * r128, last128), KV_COL_BLOCK)),
            _const_spec((1, D_MODEL)),
            _const_spec((3, 256)),
            _const_spec((256, 256)),
            _const_spec((1, 256)),
            _const_spec((1, 256)),
            _const_spec((SG_GROUPS, CHUNK, CHUNK)),
            _const_spec((CHUNK, 256)),
            pl.BlockSpec(memory_space=pltpu.SMEM),
            _const_spec((N_BRANCH, D_MODEL, D_MODEL)),
            _const_spec((N_BRANCH, D_MODEL)),
            _const_spec((N_BRANCH, BRANCH_WIDTH, D_MODEL)),
            _const_spec((D_MODEL, D_MODEL)),
            _const_spec((1, D_MODEL)),
        ],
        out_specs=pl.BlockSpec((tile, D_MODEL), lambda i: (i, 0)),
        out_shape=jax.ShapeDtypeStruct((n, D_MODEL), F32),
        scratch_shapes=[
            pltpu.VMEM((tile + 2 * HALO_ROWS, 256), F32),
            pltpu.VMEM((tile + 2 * HALO_ROWS, 256), F32),
            pltpu.VMEM((tile + HALO_ROWS, LANES), F32),
            pltpu.VMEM((tile + 2 * ATTN_BLOCK, 256), BF16),
            pltpu.VMEM((LANES, tile + 2 * ATTN_BLOCK), BF16),
            pltpu.VMEM((N_BRANCH, tile, BRANCH_WIDTH), BF16),
            pltpu.VMEM((N_KV_HEADS, 2 * ATTN_BLOCK, 3 * ATTN_BLOCK), F32),
            pltpu.VMEM((tile, D_MODEL), F32),
        ],
        compiler_params=pltpu.CompilerParams(
            dimension_semantics=("arbitrary",), vmem_limit_bytes=VMEM_LIMIT_BYTES),
        name="mix",
    )(x, z, z, z, z, z,
      p["gpre"], p["convw"], p["poolw"], p["poolscale"], p["sgnorm"], p["sgwt"], p["sgb"],
      p["sink"], p["wg"], p["bg"], p["wb"], p["wo"], p["gpost"])


def _ffn_kernel(x_ref, gpre_ref, win_ref, wout_ref, gpost_ref, o_ref):
    x = x_ref[...]
    h = _rms_norm(x, gpre_ref[...]).astype(BF16)
    y = jnp.zeros((FFN_TILE, D_MODEL), F32)
    for c in range(N_FF_CHUNKS):
        a = _dot(h, win_ref[:, c * MXU:(c + 1) * MXU])
        g = _dot(h, win_ref[:, D_FF + c * MXU:D_FF + (c + 1) * MXU])
        act = (a * jax.nn.sigmoid(a) * g).astype(BF16)
        y = y + _dot(act, wout_ref[c * MXU:(c + 1) * MXU, :])
    o_ref[...] = x + _rms_norm(y, gpost_ref[...])


def _ffn(x, p):
    n = x.shape[0]
    return pl.pallas_call(
        _ffn_kernel,
        grid=(n // FFN_TILE,),
        in_specs=[
            pl.BlockSpec((FFN_TILE, D_MODEL), lambda i: (i, 0)),
            _const_spec((1, D_MODEL)),
            _const_spec((D_MODEL, 2 * D_FF)),
            _const_spec((D_FF, D_MODEL)),
            _const_spec((1, D_MODEL)),
        ],
        out_specs=pl.BlockSpec((FFN_TILE, D_MODEL), lambda i: (i, 0)),
        out_shape=jax.ShapeDtypeStruct((n, D_MODEL), F32),
        compiler_params=pltpu.CompilerParams(
            dimension_semantics=("arbitrary",), vmem_limit_bytes=VMEM_LIMIT_BYTES),
        name="ffn",
    )(x, p["ffn_gpre"], p["ffn_win"], p["ffn_wout"], p["ffn_gpost"])


def _layer_params(l, norm_mix_pre, norm_mix_post, norm_ffn_pre, norm_ffn_post, w_in, conv_w,
                  pool_w, pool_scale, sg_norm, sg_w, sg_b, attn_sink, w_branch, w_gate, b_gate,
                  w_out, w_ffn_in, w_ffn_out):
    poolw = jnp.zeros((256, 256), F32)
    for g in range(256 // POOL_GDIM):
        poolw = poolw.at[g * POOL_GDIM:(g + 1) * POOL_GDIM,
                         g * POOL_GDIM:(g + 1) * POOL_GDIM].set(pool_w[l, g])
    return dict(
        w_in=w_in[l].astype(BF16),
        gpre=norm_mix_pre[l].reshape(1, D_MODEL),
        gpost=norm_mix_post[l].reshape(1, D_MODEL),
        convw=conv_w[l],
        poolw=poolw.astype(BF16),
        poolscale=pool_scale[l].reshape(1, 256),
        sgnorm=sg_norm[l].reshape(1, 256),
        sgwt=jnp.swapaxes(sg_w[l], 1, 2).astype(BF16),
        sgb=jnp.repeat(sg_b[l].T, SG_GDIM, axis=1),
        sink=attn_sink[l],
        wg=w_gate[l].astype(BF16),
        bg=b_gate[l],
        wb=w_branch[l].astype(BF16),
        wo=w_out[l].astype(BF16),
        ffn_gpre=norm_ffn_pre[l].reshape(1, D_MODEL),
        ffn_gpost=norm_ffn_post[l].reshape(1, D_MODEL),
        ffn_win=w_ffn_in[l].astype(BF16),
        ffn_wout=w_ffn_out[l].astype(BF16),
    )


def _trunk(x, layers):
    b, s, d = x.shape
    x = x.reshape(b * s, d)
    for p in layers:
        z = _in_proj(x, p["gpre"], p["w_in"])
        x = _mix(x, z, p, s)
        x = _ffn(x, p)
    return x.reshape(b, s, d)


def kernel(x_prompt, x_sample, norm_mix_pre, norm_mix_post, norm_ffn_pre, norm_ffn_post, w_in,
           conv_w, pool_w, pool_scale, sg_norm, sg_w, sg_b, attn_sink, w_branch, w_gate, b_gate,
           w_out, w_ffn_in, w_ffn_out):
    depth = w_in.shape[0]
    layers = [
        _layer_params(l, norm_mix_pre, norm_mix_post, norm_ffn_pre, norm_ffn_post, w_in, conv_w,
                      pool_w, pool_scale, sg_norm, sg_w, sg_b, attn_sink, w_branch, w_gate,
                      b_gate, w_out, w_ffn_in, w_ffn_out)
        for l in range(depth)]
    return (_trunk(x_prompt, layers), _trunk(x_sample, layers))
```

```python
import functools

import jax
import jax.numpy as jnp
from jax import lax
from jax.experimental import pallas as pl
from jax.experimental.pallas import tpu as pltpu

D_MODEL = 1024
EPS = 1e-6
BRANCH_WIDTH = 256
N_BRANCH = 4
POOL_GDIM = 64
CHUNK = 128
SG_GROUPS = 4
SG_GDIM = 64
N_Q_HEADS = 4
N_KV_HEADS = 2
HEAD_DIM = 64
WINDOW = 128
ATTN_BLOCK = 128
NEG_BIG = -1e30
D_IN = 2048
D_FF = 2816

COL_BGATE, COL_CGATE, COL_XIN, COL_POOL = 0, 256, 512, 768
COL_SG_U, COL_SG_V, COL_Q, COL_K, COL_V = 1024, 1280, 1536, 1792, 1920
KV_COL_BLOCK = COL_K // 256

LANES = 128
MXU = 256
HALO_ROWS = 16
N_COL_CHUNKS = D_MODEL // MXU
N_FF_CHUNKS = D_FF // MXU
TILE = 512
IN_TILE = 2048
FFN_TILE = 1024
SUB_TILE = 512
VMEM_LIMIT_BYTES = 56 * 1024 * 1024

BF16 = jnp.bfloat16
F32 = jnp.float32


def _dot(a, b):
    return jnp.dot(a, b, preferred_element_type=F32)


def _rms_norm(x, gain):
    ms = jnp.mean(x * x, axis=-1, keepdims=True)
    return x * lax.rsqrt(ms + EPS) * gain


def _const_spec(shape):
    nd = len(shape)
    return pl.BlockSpec(shape, lambda i: (0,) * nd, pipeline_mode=pl.Buffered(1))


def _in_proj_kernel(x_ref, gain_ref, w_ref, h_ref, z_ref):
    for r in range(IN_TILE // SUB_TILE):
        rows = slice(r * SUB_TILE, (r + 1) * SUB_TILE)
        h = _rms_norm(x_ref[rows, :], gain_ref[...]).astype(BF16)
        h_ref[rows, :] = h
        z_ref[rows, :] = _dot(h, w_ref[...]).astype(BF16)


def _in_proj(x, gain, w_in):
    n = x.shape[0]
    return pl.pallas_call(
        _in_proj_kernel,
        grid=(n // IN_TILE,),
        in_specs=[
            pl.BlockSpec((IN_TILE, D_MODEL), lambda i: (i, 0)),
            _const_spec((1, D_MODEL)),
            _const_spec((D_MODEL, D_IN)),
        ],
        out_specs=[pl.BlockSpec((IN_TILE, D_MODEL), lambda i: (i, 0)),
                   pl.BlockSpec((IN_TILE, D_IN), lambda i: (i, 0))],
        out_shape=[jax.ShapeDtypeStruct((n, D_MODEL), BF16),
                   jax.ShapeDtypeStruct((n, D_IN), BF16)],
        compiler_params=pltpu.CompilerParams(
            dimension_semantics=("arbitrary",), vmem_limit_bytes=VMEM_LIMIT_BYTES),
        name="in_proj",
    )(x, gain, w_in)


def _attn_bias(kh):
    row = lax.broadcasted_iota(jnp.int32, (2 * ATTN_BLOCK, 3 * ATTN_BLOCK), 0)
    col = lax.broadcasted_iota(jnp.int32, (2 * ATTN_BLOCK, 3 * ATTN_BLOCK), 1)
    qi = jnp.where(row >= ATTN_BLOCK, row - ATTN_BLOCK, row)
    rel = jnp.abs(qi - col + ATTN_BLOCK)
    slope_lo = 2.0 ** (-8.0 * (2 * kh + 1) / N_Q_HEADS)
    slope_hi = 2.0 ** (-8.0 * (2 * kh + 2) / N_Q_HEADS)
    slope = jnp.where(row >= ATTN_BLOCK, slope_hi, slope_lo)
    return jnp.where(rel <= WINDOW, -slope * rel.astype(F32), NEG_BIG)


def _mix_kernel(x_ref, h_ref, z_ref, zprev_ref, znext_ref, kvprev_ref, kvnext_ref,
                convw_ref, poolw_ref, poolscale_ref, sgnorm_ref, sgwt_ref, sgb_ref,
                sink_ref, wg_ref, bg_ref, wb_ref, wo_ref, gpost_ref,
                o_ref,
                uext_ref, pext_ref, psum_ref, kvext_ref, kt_ref, br_ref, bias_ref, merged_ref,
                *, seq_len):
    tile = TILE
    i = pl.program_id(0)
    tiles_per_seq = seq_len // tile
    t_in_seq = lax.rem(i, tiles_per_seq)
    is_first = t_in_seq == 0
    is_last = t_in_seq == tiles_per_seq - 1
    n_blocks = tile // ATTN_BLOCK

    @pl.when(i == 0)
    def _():
        for kh in range(N_KV_HEADS):
            bias_ref[kh] = _attn_bias(kh)

    def attn_prepare():
        kvext_ref[0:ATTN_BLOCK, :] = kvprev_ref[...]
        kvext_ref[ATTN_BLOCK:ATTN_BLOCK + tile, :] = z_ref[:, COL_K:COL_K + 256]
        kvext_ref[ATTN_BLOCK + tile:, :] = kvnext_ref[...]
        kt_ref[...] = kvext_ref[:, 0:LANES].astype(F32).T.astype(BF16)

    col = lax.broadcasted_iota(jnp.int32, (1, 3 * ATTN_BLOCK), 1)
    row2 = lax.broadcasted_iota(jnp.int32, (2 * ATTN_BLOCK, 1), 0)

    def attn_scores(b):
        r0 = b * ATTN_BLOCK
        qb = z_ref[r0:r0 + ATTN_BLOCK, COL_Q:COL_Q + 256]
        edge = jnp.zeros((1, 3 * ATTN_BLOCK), F32)
        if b == 0:
            edge = jnp.where(jnp.logical_and(is_first, col < ATTN_BLOCK), NEG_BIG, edge)
        if b == n_blocks - 1:
            edge = jnp.where(jnp.logical_and(is_last, col >= 2 * ATTN_BLOCK), NEG_BIG, edge)
        scores = []
        for kh in range(N_KV_HEADS):
            kt = kt_ref[kh * HEAD_DIM:(kh + 1) * HEAD_DIM, r0:r0 + 3 * ATTN_BLOCK]
            q2 = jnp.concatenate(
                [qb[:, (2 * kh) * HEAD_DIM:(2 * kh + 1) * HEAD_DIM],
                 qb[:, (2 * kh + 1) * HEAD_DIM:(2 * kh + 2) * HEAD_DIM]], axis=0)
            s = _dot(q2, kt) * (HEAD_DIM ** -0.5) + bias_ref[kh]
            if b == 0 or b == n_blocks - 1:
                s = s + edge
            scores.append(s)
        return scores

    def attn_values(b, scores):
        r0 = b * ATTN_BLOCK
        outs = []
        for kh in range(N_KV_HEADS):
            s = scores[kh]
            vv = kvext_ref[r0:r0 + 3 * ATTN_BLOCK,
                           LANES + kh * HEAD_DIM:LANES + (kh + 1) * HEAD_DIM]
            sink = jnp.where(row2 < ATTN_BLOCK, sink_ref[2 * kh], sink_ref[2 * kh + 1])
            m = jnp.maximum(jnp.max(s, axis=-1, keepdims=True), sink)
            p = jnp.exp(s - m)
            denom = jnp.sum(p, axis=-1, keepdims=True) + jnp.exp(sink - m)
            o = _dot(p.astype(BF16), vv) / denom
            outs += [o[:ATTN_BLOCK], o[ATTN_BLOCK:]]
        br_ref[3, r0:r0 + ATTN_BLOCK, :] = jnp.concatenate(outs, axis=1).astype(BF16)

    def conv_branch():
        def gated_u(ref):
            c = ref[:, COL_CGATE:COL_CGATE + 256].astype(F32)
            xin = ref[:, COL_XIN:COL_XIN + 256].astype(F32)
            return c * xin

        uext_ref[0:HALO_ROWS, :] = jnp.where(is_first, 0.0, gated_u(zprev_ref))
        uext_ref[HALO_ROWS:HALO_ROWS + tile, :] = gated_u(z_ref)
        uext_ref[HALO_ROWS + tile:, :] = jnp.where(is_last, 0.0, gated_u(znext_ref))
        conv = (convw_ref[0:1, :] * uext_ref[HALO_ROWS - 1:HALO_ROWS - 1 + tile, :]
                + convw_ref[1:2, :] * uext_ref[HALO_ROWS:HALO_ROWS + tile, :]
                + convw_ref[2:3, :] * uext_ref[HALO_ROWS + 1:HALO_ROWS + 1 + tile, :])
        br_ref[0] = (z_ref[:, COL_BGATE:COL_BGATE + 256].astype(F32) * conv).astype(BF16)

    def pool_fill():
        pext_ref[0:HALO_ROWS, :] = jnp.where(
            is_first, 0.0, zprev_ref[:, COL_POOL:COL_POOL + 256].astype(F32))
        pext_ref[HALO_ROWS:HALO_ROWS + tile, :] = z_ref[:, COL_POOL:COL_POOL + 256].astype(F32)
        pext_ref[HALO_ROWS + tile:, :] = jnp.where(
            is_last, 0.0, znext_ref[:, COL_POOL:COL_POOL + 256].astype(F32))

    def pool_sums_wide():
        rows = tile + HALO_ROWS
        acc = pext_ref[HALO_ROWS - 8:HALO_ROWS - 8 + rows, LANES:]
        for back in range(1, 8):
            acc = acc + pext_ref[HALO_ROWS - back:HALO_ROWS - back + rows, LANES:]
        psum_ref[...] = acc

    def inv_count(w, row0, rows):
        pos = t_in_seq * tile + row0 + lax.broadcasted_iota(jnp.int32, (rows, LANES), 0)
        hi = jnp.minimum(pos + w // 2, seq_len)
        lo = jnp.maximum(pos - w // 2, 0)
        return 1.0 / (hi - lo).astype(F32)

    def pool_branch():
        def shifted(off, rows):
            return pext_ref[HALO_ROWS + off + rows.start:HALO_ROWS + off + rows.stop, 0:LANES]

        def pooled_rows(rows, exact_counts):
            n = rows.stop - rows.start
            lane = lax.broadcasted_iota(jnp.int32, (n, LANES), 1)
            s2 = shifted(-1, rows) + shifted(0, rows)
            s4 = s2 + shifted(-2, rows) + shifted(1, rows)
            s8 = psum_ref[rows.start + 4:rows.stop + 4, :]
            s16 = psum_ref[rows.start:rows.stop, :] + psum_ref[rows.start + 8:rows.stop + 8, :]
            if exact_counts:
                inv = [inv_count(w, rows.start, n) for w in (2, 4, 8, 16)]
            else:
                inv = [1.0 / w for w in (2, 4, 8, 16)]
            mean = jnp.concatenate(
                [jnp.where(lane < POOL_GDIM, s2 * inv[0], s4 * inv[1]),
                 jnp.where(lane < POOL_GDIM, s8 * inv[2], s16 * inv[3])], axis=1)
            return mean - pext_ref[HALO_ROWS + rows.start:HALO_ROWS + rows.stop, :]

        pooled = jnp.concatenate(
            [pooled_rows(slice(0, HALO_ROWS), True),
             pooled_rows(slice(HALO_ROWS, tile - HALO_ROWS), False),
             pooled_rows(slice(tile - HALO_ROWS, tile), True)], axis=0)
        br_ref[1] = (_dot(pooled.astype(BF16), poolw_ref[...]) * poolscale_ref[...]).astype(BF16)

    def gating_branch():
        v = z_ref[:, COL_SG_V:COL_SG_V + 256].astype(F32)
        mu = jnp.mean(v, axis=-1, keepdims=True)
        vc = v - mu
        var = jnp.mean(vc * vc, axis=-1, keepdims=True)
        vln_t = (vc * lax.rsqrt(var + EPS) * sgnorm_ref[...]).T
        n_chunks = tile // CHUNK
        mixed_t = []
        for g in range(SG_GROUPS):
            lhs = jnp.concatenate(
                [vln_t[g * SG_GDIM:(g + 1) * SG_GDIM, n * CHUNK:(n + 1) * CHUNK]
                 for n in range(n_chunks)], axis=0)
            mixed_t.append(_dot(lhs.astype(BF16), sgwt_ref[g]))
        for n in range(n_chunks):
            rows = slice(n * CHUNK, (n + 1) * CHUNK)
            tiles = []
            for j in range(2):
                piece = jnp.concatenate(
                    [mixed_t[2 * j][n * SG_GDIM:(n + 1) * SG_GDIM],
                     mixed_t[2 * j + 1][n * SG_GDIM:(n + 1) * SG_GDIM]], axis=0)
                tiles.append(piece.T)
            mixed = jnp.concatenate(tiles, axis=1) + sgb_ref[...]
            u = z_ref[rows, COL_SG_U:COL_SG_U + 256].astype(F32)
            br_ref[2, rows, :] = (u * mixed).astype(BF16)

    def gate(b, c, h):
        cols = slice(c * MXU, (c + 1) * MXU)
        return jax.nn.sigmoid(_dot(h, wg_ref[b, :, cols]) + bg_ref[b:b + 1, cols])

    def project(b, gates):
        for c in range(N_COL_CHUNKS):
            cols = slice(c * MXU, (c + 1) * MXU)
            term = gates[c] * _dot(br_ref[b], wb_ref[b, :, cols])
            if b == 0:
                merged_ref[:, cols] = term
            else:
                merged_ref[:, cols] += term

    attn_prepare()
    sc0 = attn_scores(0)
    sc1 = attn_scores(1)
    h = h_ref[...]

    g = [gate(0, 0, h)]
    attn_values(0, sc0)
    g.append(gate(0, 1, h))
    conv_branch()
    g.append(gate(0, 2, h))
    g.append(gate(0, 3, h))
    project(0, g)

    g = [gate(2, 0, h)]
    gating_branch()
    g.append(gate(2, 1, h))
    attn_values(1, sc1)
    g.append(gate(2, 2, h))
    pool_fill()
    pool_sums_wide()
    g.append(gate(2, 3, h))
    project(2, g)

    g = [gate(1, 0, h)]
    pool_branch()
    g.append(gate(1, 1, h))
    sc2 = attn_scores(2)
    g.append(gate(1, 2, h))
    attn_values(2, sc2)
    g.append(gate(1, 3, h))
    project(1, g)

    g = [gate(3, 0, h)]
    sc3 = attn_scores(3)
    g.append(gate(3, 1, h))
    attn_values(3, sc3)
    g.append(gate(3, 2, h))
    g.append(gate(3, 3, h))
    project(3, g)

    for r in range(2):
        rows = slice(r * (tile // 2), (r + 1) * (tile // 2))
        y = _dot(merged_ref[rows, :].astype(BF16), wo_ref[...])
        o_ref[rows, :] = x_ref[rows, :] + _rms_norm(y, gpost_ref[...])


def _mix(x, h, z, p, seq_len):
    n = x.shape[0]
    tile = TILE
    assert tile == N_BRANCH * ATTN_BLOCK and seq_len % tile == 0
    r16 = tile // HALO_ROWS
    r128 = tile // ATTN_BLOCK
    last16 = n // HALO_ROWS - 1
    last128 = n // ATTN_BLOCK - 1
    kernel = functools.partial(_mix_kernel, seq_len=seq_len)
    return pl.pallas_call(
        kernel,
        grid=(n // tile,),
        in_specs=[
            pl.BlockSpec((tile, D_MODEL), lambda i: (i, 0)),
            pl.BlockSpec((tile, D_MODEL), lambda i: (i, 0)),
            pl.BlockSpec((tile, D_IN), lambda i: (i, 0)),
            pl.BlockSpec((HALO_ROWS, 1024), lambda i: (jnp.maximum(i * r16 - 1, 0), 0)),
            pl.BlockSpec((HALO_ROWS, 1024), lambda i: (jnp.minimum((i + 1) * r16, last16), 0)),
            pl.BlockSpec((ATTN_BLOCK, 256),
                         lambda i: (jnp.maximum(i * r128 - 1, 0), KV_COL_BLOCK)),
            pl.BlockSpec((ATTN_BLOCK, 256),
                         lambda i: (jnp.minimum((i + 1) * r128, last128), KV_COL_BLOCK)),
            _const_spec((3, 256)),
            _const_spec((256, 256)),
            _const_spec((1, 256)),
            _const_spec((1, 256)),
            _const_spec((SG_GROUPS, CHUNK, CHUNK)),
            _const_spec((CHUNK, 256)),
            pl.BlockSpec(memory_space=pltpu.SMEM),
            _const_spec((N_BRANCH, D_MODEL, D_MODEL)),
            _const_spec((N_BRANCH, D_MODEL)),
            _const_spec((N_BRANCH, BRANCH_WIDTH, D_MODEL)),
            _const_spec((D_MODEL, D_MODEL)),
            _const_spec((1, D_MODEL)),
        ],
        out_specs=pl.BlockSpec((tile, D_MODEL), lambda i: (i, 0)),
        out_shape=jax.ShapeDtypeStruct((n, D_MODEL), F32),
        scratch_shapes=[
            pltpu.VMEM((tile + 2 * HALO_ROWS, 256), F32),
            pltpu.VMEM((tile + 2 * HALO_ROWS, 256), F32),
            pltpu.VMEM((tile + HALO_ROWS, LANES), F32),
            pltpu.VMEM((tile + 2 * ATTN_BLOCK, 256), BF16),
            pltpu.VMEM((LANES, tile + 2 * ATTN_BLOCK), BF16),
            pltpu.VMEM((N_BRANCH, tile, BRANCH_WIDTH), BF16),
            pltpu.VMEM((N_KV_HEADS, 2 * ATTN_BLOCK, 3 * ATTN_BLOCK), F32),
            pltpu.VMEM((tile, D_MODEL), F32),
        ],
        compiler_params=pltpu.CompilerParams(
            dimension_semantics=("arbitrary",), vmem_limit_bytes=VMEM_LIMIT_BYTES),
        name="mix",
    )(x, h, z, z, z, z, z,
      p["convw"], p["poolw"], p["poolscale"], p["sgnorm"], p["sgwt"], p["sgb"],
      p["sink"], p["wg"], p["bg"], p["wb"], p["wo"], p["gpost"])


def _ffn_kernel(x_ref, gpre_ref, win_ref, wout_ref, gpost_ref, o_ref):
    for r in range(FFN_TILE // SUB_TILE):
        rows = slice(r * SUB_TILE, (r + 1) * SUB_TILE)
        x = x_ref[rows, :]
        h = _rms_norm(x, gpre_ref[...]).astype(BF16)
        y = jnp.zeros((SUB_TILE, D_MODEL), F32)
        for c in range(N_FF_CHUNKS):
            a = _dot(h, win_ref[:, c * MXU:(c + 1) * MXU])
            g = _dot(h, win_ref[:, D_FF + c * MXU:D_FF + (c + 1) * MXU])
            act = (a * jax.nn.sigmoid(a) * g).astype(BF16)
            y = y + _dot(act, wout_ref[c * MXU:(c + 1) * MXU, :])
        o_ref[rows, :] = x + _rms_norm(y, gpost_ref[...])


def _ffn(x, p):
    n = x.shape[0]
    return pl.pallas_call(
        _ffn_kernel,
        grid=(n // FFN_TILE,),
        in_specs=[
            pl.BlockSpec((FFN_TILE, D_MODEL), lambda i: (i, 0)),
            _const_spec((1, D_MODEL)),
            _const_spec((D_MODEL, 2 * D_FF)),
            _const_spec((D_FF, D_MODEL)),
            _const_spec((1, D_MODEL)),
        ],
        out_specs=pl.BlockSpec((FFN_TILE, D_MODEL), lambda i: (i, 0)),
        out_shape=jax.ShapeDtypeStruct((n, D_MODEL), F32),
        compiler_params=pltpu.CompilerParams(
            dimension_semantics=("arbitrary",), vmem_limit_bytes=VMEM_LIMIT_BYTES),
        name="ffn",
    )(x, p["ffn_gpre"], p["ffn_win"], p["ffn_wout"], p["ffn_gpost"])


def _layer_params(l, norm_mix_pre, norm_mix_post, norm_ffn_pre, norm_ffn_post, w_in, conv_w,
                  pool_w, pool_scale, sg_norm, sg_w, sg_b, attn_sink, w_branch, w_gate, b_gate,
                  w_out, w_ffn_in, w_ffn_out):
    poolw = jnp.zeros((256, 256), F32)
    for g in range(256 // POOL_GDIM):
        poolw = poolw.at[g * POOL_GDIM:(g + 1) * POOL_GDIM,
                         g * POOL_GDIM:(g + 1) * POOL_GDIM].set(pool_w[l, g])
    return dict(
        w_in=w_in[l].astype(BF16),
        gpre=norm_mix_pre[l].reshape(1, D_MODEL),
        gpost=norm_mix_post[l].reshape(1, D_MODEL),
        convw=conv_w[l],
        poolw=poolw.astype(BF16),
        poolscale=pool_scale[l].reshape(1, 256),
        sgnorm=sg_norm[l].reshape(1, 256),
        sgwt=jnp.swapaxes(sg_w[l], 1, 2).astype(BF16),
        sgb=jnp.repeat(sg_b[l].T, SG_GDIM, axis=1),
        sink=attn_sink[l],
        wg=w_gate[l].astype(BF16),
        bg=b_gate[l],
        wb=w_branch[l].astype(BF16),
        wo=w_out[l].astype(BF16),
        ffn_gpre=norm_ffn_pre[l].reshape(1, D_MODEL),
        ffn_gpost=norm_ffn_post[l].reshape(1, D_MODEL),
        ffn_win=w_ffn_in[l].astype(BF16),
        ffn_wout=w_ffn_out[l].astype(BF16),
    )


def _trunk(x, layers):
    b, s, d = x.shape
    x = x.reshape(b * s, d)
    for p in layers:
        h, z = _in_proj(x, p["gpre"], p["w_in"])
        x = _mix(x, h, z, p, s)
        x = _ffn(x, p)
    return x.reshape(b, s, d)


def kernel(x_prompt, x_sample, norm_mix_pre, norm_mix_post, norm_ffn_pre, norm_ffn_post, w_in,
           conv_w, pool_w, pool_scale, sg_norm, sg_w, sg_b, attn_sink, w_branch, w_gate, b_gate,
           w_out, w_ffn_in, w_ffn_out):
    depth = w_in.shape[0]
    layers = [
        _layer_params(l, norm_mix_pre, norm_mix_post, norm_ffn_pre, norm_ffn_post, w_in, conv_w,
                      pool_w, pool_scale, sg_norm, sg_w, sg_b, attn_sink, w_branch, w_gate,
                      b_gate, w_out, w_ffn_in, w_ffn_out)
        for l in range(depth)]
    return (_trunk(x_prompt, layers), _trunk(x_sample, layers))
```

```python
import functools

import jax
import jax.numpy as jnp
from jax import lax
from jax.experimental import pallas as pl
from jax.experimental.pallas import tpu as pltpu

D_MODEL = 1024
EPS = 1e-6
BRANCH_WIDTH = 256
N_BRANCH = 4
POOL_GDIM = 64
CHUNK = 128
SG_GROUPS = 4
SG_GDIM = 64
N_Q_HEADS = 4
N_KV_HEADS = 2
HEAD_DIM = 64
WINDOW = 128
ATTN_BLOCK = 128
NEG_BIG = -1e30
D_IN = 2048
D_FF = 2816

COL_BGATE, COL_CGATE, COL_XIN, COL_POOL = 0, 256, 512, 768
COL_SG_U, COL_SG_V, COL_Q, COL_K, COL_V = 1024, 1280, 1536, 1792, 1920
KV_COL_BLOCK = COL_K // 256

LANES = 128
MXU = 256
HALO_ROWS = 16
N_COL_CHUNKS = D_MODEL // MXU
N_FF_CHUNKS = D_FF // MXU
TILE = 512
IN_TILE = 2048
FFN_TILE = 1024
SUB_TILE = 512
VMEM_LIMIT_BYTES = 56 * 1024 * 1024

BF16 = jnp.bfloat16
F32 = jnp.float32


def _dot(a, b):
    return jnp.dot(a, b, preferred_element_type=F32)


def _rms_norm(x, gain):
    ms = jnp.mean(x * x, axis=-1, keepdims=True)
    return x * lax.rsqrt(ms + EPS) * gain


def _const_spec(shape):
    nd = len(shape)
    return pl.BlockSpec(shape, lambda i: (0,) * nd, pipeline_mode=pl.Buffered(1))


def _in_proj_kernel(x_ref, gain_ref, w_ref, h_ref, z_ref):
    for r in range(IN_TILE // SUB_TILE):
        rows = slice(r * SUB_TILE, (r + 1) * SUB_TILE)
        h = _rms_norm(x_ref[rows, :], gain_ref[...]).astype(BF16)
        h_ref[rows, :] = h
        z_ref[rows, :] = _dot(h, w_ref[...]).astype(BF16)


def _in_proj(x, gain, w_in):
    n = x.shape[0]
    return pl.pallas_call(
        _in_proj_kernel,
        grid=(n // IN_TILE,),
        in_specs=[
            pl.BlockSpec((IN_TILE, D_MODEL), lambda i: (i, 0)),
            _const_spec((1, D_MODEL)),
            _const_spec((D_MODEL, D_IN)),
        ],
        out_specs=[pl.BlockSpec((IN_TILE, D_MODEL), lambda i: (i, 0)),
                   pl.BlockSpec((IN_TILE, D_IN), lambda i: (i, 0))],
        out_shape=[jax.ShapeDtypeStruct((n, D_MODEL), BF16),
                   jax.ShapeDtypeStruct((n, D_IN), BF16)],
        compiler_params=pltpu.CompilerParams(
            dimension_semantics=("arbitrary",), vmem_limit_bytes=VMEM_LIMIT_BYTES),
        name="in_proj",
    )(x, gain, w_in)


def _attn_bias_t():
    key = lax.broadcasted_iota(jnp.int32, (3 * ATTN_BLOCK, N_Q_HEADS * ATTN_BLOCK), 0)
    col = lax.broadcasted_iota(jnp.int32, (3 * ATTN_BLOCK, N_Q_HEADS * ATTN_BLOCK), 1)
    head = sum((col >= hd * ATTN_BLOCK).astype(jnp.int32) for hd in range(1, N_Q_HEADS))
    rel = jnp.abs(col - head * ATTN_BLOCK - key + ATTN_BLOCK)
    slope = jnp.full(key.shape, 2.0 ** (-8.0 / N_Q_HEADS), F32)
    for hd in range(1, N_Q_HEADS):
        slope = jnp.where(head == hd, 2.0 ** (-8.0 * (hd + 1) / N_Q_HEADS), slope)
    return jnp.where(rel <= WINDOW, -slope * rel.astype(F32), NEG_BIG)


def _mix_kernel(x_ref, h_ref, z_ref, zprev_ref, znext_ref, kvprev_ref, kvnext_ref,
                convw_ref, poolw_ref, poolscale_ref, sgnorm_ref, sgwt_ref, sgb_ref,
                sink_ref, wg_ref, bg_ref, wb_ref, wo_ref, gpost_ref,
                o_ref,
                uext_ref, pext_ref, psum_ref, kvext_ref, vt_ref, qt_ref, br_ref, bias_ref,
                merged_ref,
                *, seq_len):
    tile = TILE
    i = pl.program_id(0)
    tiles_per_seq = seq_len // tile
    t_in_seq = lax.rem(i, tiles_per_seq)
    is_first = t_in_seq == 0
    is_last = t_in_seq == tiles_per_seq - 1
    n_blocks = tile // ATTN_BLOCK

    @pl.when(i == 0)
    def _():
        bias_ref[...] = _attn_bias_t()

    def attn_prepare():
        kvext_ref[0:ATTN_BLOCK, :] = kvprev_ref[...]
        kvext_ref[ATTN_BLOCK:ATTN_BLOCK + tile, :] = z_ref[:, COL_K:COL_K + 256]
        kvext_ref[ATTN_BLOCK + tile:, :] = kvnext_ref[...]
        vt_ref[...] = kvext_ref[:, LANES:].astype(F32).T.astype(BF16)
        qt_ref[...] = z_ref[:, COL_Q:COL_Q + 256].astype(F32).T.astype(BF16)

    key_row = lax.broadcasted_iota(jnp.int32, (3 * ATTN_BLOCK, N_Q_HEADS * ATTN_BLOCK), 0)
    head_col = lax.broadcasted_iota(jnp.int32, (1, N_Q_HEADS * ATTN_BLOCK), 1)
    sink_row = jnp.full((1, N_Q_HEADS * ATTN_BLOCK), sink_ref[0], F32)
    for hd in range(1, N_Q_HEADS):
        sink_row = jnp.where(head_col >= hd * ATTN_BLOCK, sink_ref[hd], sink_row)

    def attn_scores(b):
        r0 = b * ATTN_BLOCK
        keys = kvext_ref[r0:r0 + 3 * ATTN_BLOCK, 0:LANES]
        qt = [qt_ref[hd * HEAD_DIM:(hd + 1) * HEAD_DIM, r0:r0 + ATTN_BLOCK]
              for hd in range(N_Q_HEADS)]
        zero = jnp.zeros((HEAD_DIM, ATTN_BLOCK), BF16)
        rhs = jnp.concatenate(
            [jnp.concatenate([qt[0], qt[1], zero, zero], axis=1),
             jnp.concatenate([zero, zero, qt[2], qt[3]], axis=1)], axis=0)
        s = _dot(keys, rhs) * (HEAD_DIM ** -0.5) + bias_ref[...]
        if b == 0:
            s = s + jnp.where(jnp.logical_and(is_first, key_row < ATTN_BLOCK), NEG_BIG, 0.0)
        if b == n_blocks - 1:
            s = s + jnp.where(jnp.logical_and(is_last, key_row >= 2 * ATTN_BLOCK), NEG_BIG, 0.0)
        return s

    def attn_values(b, s):
        r0 = b * ATTN_BLOCK
        m = jnp.maximum(jnp.max(s, axis=0, keepdims=True), sink_row)
        p = jnp.exp(s - m)
        denom = jnp.sum(p, axis=0, keepdims=True) + jnp.exp(sink_row - m)
        p = p.astype(BF16)
        for kh in range(N_KV_HEADS):
            cols = slice(kh * 2 * ATTN_BLOCK, (kh + 1) * 2 * ATTN_BLOCK)
            vt = vt_ref[kh * HEAD_DIM:(kh + 1) * HEAD_DIM, r0:r0 + 3 * ATTN_BLOCK]
            o_t = _dot(vt, p[:, cols]) / denom[:, cols]
            both = jnp.concatenate([o_t[:, :ATTN_BLOCK], o_t[:, ATTN_BLOCK:]], axis=0)
            br_ref[3, r0:r0 + ATTN_BLOCK, kh * LANES:(kh + 1) * LANES] = both.T.astype(BF16)

    def conv_fill():
        def gated_u(ref):
            c = ref[:, COL_CGATE:COL_CGATE + 256].astype(F32)
            xin = ref[:, COL_XIN:COL_XIN + 256].astype(F32)
            return c * xin

        uext_ref[0:HALO_ROWS, :] = jnp.where(is_first, 0.0, gated_u(zprev_ref))
        uext_ref[HALO_ROWS:HALO_ROWS + tile, :] = gated_u(z_ref)
        uext_ref[HALO_ROWS + tile:, :] = jnp.where(is_last, 0.0, gated_u(znext_ref))

    def conv_rows(half):
        n = tile // 2
        r0 = half * n
        conv = (convw_ref[0:1, :] * uext_ref[HALO_ROWS - 1 + r0:HALO_ROWS - 1 + r0 + n, :]
                + convw_ref[1:2, :] * uext_ref[HALO_ROWS + r0:HALO_ROWS + r0 + n, :]
                + convw_ref[2:3, :] * uext_ref[HALO_ROWS + 1 + r0:HALO_ROWS + 1 + r0 + n, :])
        bgate = z_ref[r0:r0 + n, COL_BGATE:COL_BGATE + 256].astype(F32)
        br_ref[0, r0:r0 + n, :] = (bgate * conv).astype(BF16)

    def pool_fill():
        pext_ref[0:HALO_ROWS, :] = jnp.where(
            is_first, 0.0, zprev_ref[:, COL_POOL:COL_POOL + 256].astype(F32))
        pext_ref[HALO_ROWS:HALO_ROWS + tile, :] = z_ref[:, COL_POOL:COL_POOL + 256].astype(F32)
        pext_ref[HALO_ROWS + tile:, :] = jnp.where(
            is_last, 0.0, znext_ref[:, COL_POOL:COL_POOL + 256].astype(F32))

    def pool_sums_wide(half):
        rows = (tile + HALO_ROWS) // 2
        r0 = half * rows
        acc = pext_ref[HALO_ROWS - 8 + r0:HALO_ROWS - 8 + r0 + rows, LANES:]
        for back in range(1, 8):
            acc = acc + pext_ref[HALO_ROWS - back + r0:HALO_ROWS - back + r0 + rows, LANES:]
        psum_ref[r0:r0 + rows, :] = acc

    def inv_count(w, row0, rows):
        pos = t_in_seq * tile + row0 + lax.broadcasted_iota(jnp.int32, (rows, LANES), 0)
        hi = jnp.minimum(pos + w // 2, seq_len)
        lo = jnp.maximum(pos - w // 2, 0)
        return 1.0 / (hi - lo).astype(F32)

    def pool_branch(half):
        def shifted(off, rows):
            return pext_ref[HALO_ROWS + off + rows.start:HALO_ROWS + off + rows.stop, 0:LANES]

        def pooled_rows(rows, exact_counts):
            n = rows.stop - rows.start
            lane = lax.broadcasted_iota(jnp.int32, (n, LANES), 1)
            s2 = shifted(-1, rows) + shifted(0, rows)
            s4 = s2 + shifted(-2, rows) + shifted(1, rows)
            s8 = psum_ref[rows.start + 4:rows.stop + 4, :]
            s16 = psum_ref[rows.start:rows.stop, :] + psum_ref[rows.start + 8:rows.stop + 8, :]
            if exact_counts:
                inv = [inv_count(w, rows.start, n) for w in (2, 4, 8, 16)]
            else:
                inv = [1.0 / w for w in (2, 4, 8, 16)]
            mean = jnp.concatenate(
                [jnp.where(lane < POOL_GDIM, s2 * inv[0], s4 * inv[1]),
                 jnp.where(lane < POOL_GDIM, s8 * inv[2], s16 * inv[3])], axis=1)
            return mean - pext_ref[HALO_ROWS + rows.start:HALO_ROWS + rows.stop, :]

        mid = tile // 2
        if half == 0:
            rows = slice(0, mid)
            pooled = jnp.concatenate(
                [pooled_rows(slice(0, HALO_ROWS), True),
                 pooled_rows(slice(HALO_ROWS, mid), False)], axis=0)
        else:
            rows = slice(mid, tile)
            pooled = jnp.concatenate(
                [pooled_rows(slice(mid, tile - HALO_ROWS), False),
                 pooled_rows(slice(tile - HALO_ROWS, tile), True)], axis=0)
        br_ref[1, rows, :] = (
            _dot(pooled.astype(BF16), poolw_ref[...]) * poolscale_ref[...]).astype(BF16)

    def gating_norm():
        v = z_ref[:, COL_SG_V:COL_SG_V + 256].astype(F32)
        mu = jnp.mean(v, axis=-1, keepdims=True)
        vc = v - mu
        var = jnp.mean(vc * vc, axis=-1, keepdims=True)
        return (vc * lax.rsqrt(var + EPS) * sgnorm_ref[...]).T

    def gating_mix(vln_t):
        n_chunks = tile // CHUNK
        mixed_t = []
        for g in range(SG_GROUPS):
            lhs = jnp.concatenate(
                [vln_t[g * SG_GDIM:(g + 1) * SG_GDIM, n * CHUNK:(n + 1) * CHUNK]
                 for n in range(n_chunks)], axis=0)
            mixed_t.append(_dot(lhs.astype(BF16), sgwt_ref[g]))
        for n in range(n_chunks):
            rows = slice(n * CHUNK, (n + 1) * CHUNK)
            tiles = []
            for j in range(2):
                piece = jnp.concatenate(
                    [mixed_t[2 * j][n * SG_GDIM:(n + 1) * SG_GDIM],
                     mixed_t[2 * j + 1][n * SG_GDIM:(n + 1) * SG_GDIM]], axis=0)
                tiles.append(piece.T)
            mixed = jnp.concatenate(tiles, axis=1) + sgb_ref[...]
            u = z_ref[rows, COL_SG_U:COL_SG_U + 256].astype(F32)
            br_ref[2, rows, :] = (u * mixed).astype(BF16)

    def gate(b, c, h):
        cols = slice(c * MXU, (c + 1) * MXU)
        return jax.nn.sigmoid(_dot(h, wg_ref[b, :, cols]) + bg_ref[b:b + 1, cols])

    def project(b, gates):
        for c in range(N_COL_CHUNKS):
            cols = slice(c * MXU, (c + 1) * MXU)
            term = gates[c] * _dot(br_ref[b], wb_ref[b, :, cols])
            if b == 0:
                merged_ref[:, cols] = term
            else:
                merged_ref[:, cols] += term

    def finish(r):
        rows = slice(r * (tile // 2), (r + 1) * (tile // 2))
        y = jnp.zeros((tile // 2, D_MODEL), F32)
        for c in range(N_COL_CHUNKS):
            y = y + _dot(merged_ref[rows, c * MXU:(c + 1) * MXU].astype(BF16), wo_ref[c])
        o_ref[rows, :] = x_ref[rows, :] + _rms_norm(y, gpost_ref[...])

    attn_prepare()
    sc0 = attn_scores(0)
    sc1 = attn_scores(1)
    h = h_ref[...]

    g = [gate(0, 0, h)]
    conv_fill()
    g.append(gate(0, 1, h))
    conv_rows(0)
    g.append(gate(0, 2, h))
    conv_rows(1)
    g.append(gate(0, 3, h))
    attn_values(0, sc0)
    project(0, g)

    g = [gate(2, 0, h)]
    vln_t = gating_norm()
    g.append(gate(2, 1, h))
    gating_mix(vln_t)
    g.append(gate(2, 2, h))
    pool_fill()
    g.append(gate(2, 3, h))
    attn_values(1, sc1)
    project(2, g)

    g = [gate(1, 0, h)]
    pool_sums_wide(0)
    pool_sums_wide(1)
    g.append(gate(1, 1, h))
    pool_branch(0)
    sc2 = attn_scores(2)
    g.append(gate(1, 2, h))
    pool_branch(1)
    g.append(gate(1, 3, h))
    attn_values(2, sc2)
    project(1, g)

    g = [gate(3, 0, h)]
    sc3 = attn_scores(3)
    g.append(gate(3, 1, h))
    attn_values(3, sc3)
    g.append(gate(3, 2, h))
    g.append(gate(3, 3, h))
    project(3, g)

    finish(0)
    finish(1)


def _mix(x, h, z, p, seq_len):
    n = x.shape[0]
    tile = TILE
    assert tile == N_BRANCH * ATTN_BLOCK and seq_len % tile == 0
    r16 = tile // HALO_ROWS
    r128 = tile // ATTN_BLOCK
    last16 = n // HALO_ROWS - 1
    last128 = n // ATTN_BLOCK - 1
    kernel = functools.partial(_mix_kernel, seq_len=seq_len)
    return pl.pallas_call(
        kernel,
        grid=(n // tile,),
        in_specs=[
            pl.BlockSpec((tile, D_MODEL), lambda i: (i, 0)),
            pl.BlockSpec((tile, D_MODEL), lambda i: (i, 0)),
            pl.BlockSpec((tile, D_IN), lambda i: (i, 0)),
            pl.BlockSpec((HALO_ROWS, 1024), lambda i: (jnp.maximum(i * r16 - 1, 0), 0)),
            pl.BlockSpec((HALO_ROWS, 1024), lambda i: (jnp.minimum((i + 1) * r16, last16), 0)),
            pl.BlockSpec((ATTN_BLOCK, 256),
                         lambda i: (jnp.maximum(i * r128 - 1, 0), KV_COL_BLOCK)),
            pl.BlockSpec((ATTN_BLOCK, 256),
                         lambda i: (jnp.minimum((i + 1) * r128, last128), KV_COL_BLOCK)),
            _const_spec((3, 256)),
            _const_spec((256, 256)),
            _const_spec((1, 256)),
            _const_spec((1, 256)),
            _const_spec((SG_GROUPS, CHUNK, CHUNK)),
            _const_spec((CHUNK, 256)),
            pl.BlockSpec(memory_space=pltpu.SMEM),
            _const_spec((N_BRANCH, D_MODEL, D_MODEL)),
            _const_spec((N_BRANCH, D_MODEL)),
            _const_spec((N_BRANCH, BRANCH_WIDTH, D_MODEL)),
            _const_spec((N_COL_CHUNKS, MXU, D_MODEL)),
            _const_spec((1, D_MODEL)),
        ],
        out_specs=pl.BlockSpec((tile, D_MODEL), lambda i: (i, 0)),
        out_shape=jax.ShapeDtypeStruct((n, D_MODEL), F32),
        scratch_shapes=[
            pltpu.VMEM((tile + 2 * HALO_ROWS, 256), F32),
            pltpu.VMEM((tile + 2 * HALO_ROWS, 256), F32),
            pltpu.VMEM((tile + HALO_ROWS, LANES), F32),
            pltpu.VMEM((tile + 2 * ATTN_BLOCK, 256), BF16),
            pltpu.VMEM((LANES, tile + 2 * ATTN_BLOCK), BF16),
            pltpu.VMEM((N_Q_HEADS * HEAD_DIM, tile), BF16),
            pltpu.VMEM((N_BRANCH, tile, BRANCH_WIDTH), BF16),
            pltpu.VMEM((3 * ATTN_BLOCK, N_Q_HEADS * ATTN_BLOCK), F32),
            pltpu.VMEM((tile, D_MODEL), F32),
        ],
        compiler_params=pltpu.CompilerParams(
            dimension_semantics=("arbitrary",), vmem_limit_bytes=VMEM_LIMIT_BYTES),
        name="mix",
    )(x, h, z, z, z, z, z,
      p["convw"], p["poolw"], p["poolscale"], p["sgnorm"], p["sgwt"], p["sgb"],
      p["sink"], p["wg"], p["bg"], p["wb"], p["wo"], p["gpost"])


def _ffn_kernel(x_ref, gpre_ref, win_ref, wout_ref, gpost_ref, o_ref, act_ref):
    for r in range(FFN_TILE // SUB_TILE):
        rows = slice(r * SUB_TILE, (r + 1) * SUB_TILE)
        x = x_ref[rows, :]
        h = _rms_norm(x, gpre_ref[...]).astype(BF16)
        for c in range(N_FF_CHUNKS):
            a = _dot(h, win_ref[:, c * MXU:(c + 1) * MXU])
            g = _dot(h, win_ref[:, D_FF + c * MXU:D_FF + (c + 1) * MXU])
            act_ref[r, :, c * MXU:(c + 1) * MXU] = (a * jax.nn.sigmoid(a) * g).astype(BF16)
        y = _dot(act_ref[r], wout_ref[...])
        o_ref[rows, :] = x + _rms_norm(y, gpost_ref[...])


def _ffn(x, p):
    n = x.shape[0]
    return pl.pallas_call(
        _ffn_kernel,
        grid=(n // FFN_TILE,),
        in_specs=[
            pl.BlockSpec((FFN_TILE, D_MODEL), lambda i: (i, 0)),
            _const_spec((1, D_MODEL)),
            _const_spec((D_MODEL, 2 * D_FF)),
            _const_spec((D_FF, D_MODEL)),
            _const_spec((1, D_MODEL)),
        ],
        out_specs=pl.BlockSpec((FFN_TILE, D_MODEL), lambda i: (i, 0)),
        out_shape=jax.ShapeDtypeStruct((n, D_MODEL), F32),
        scratch_shapes=[pltpu.VMEM((FFN_TILE // SUB_TILE, SUB_TILE, D_FF), BF16)],
        compiler_params=pltpu.CompilerParams(
            dimension_semantics=("arbitrary",), vmem_limit_bytes=VMEM_LIMIT_BYTES),
        name="ffn",
    )(x, p["ffn_gpre"], p["ffn_win"], p["ffn_wout"], p["ffn_gpost"])


def _layer_params(l, norm_mix_pre, norm_mix_post, norm_ffn_pre, norm_ffn_post, w_in, conv_w,
                  pool_w, pool_scale, sg_norm, sg_w, sg_b, attn_sink, w_branch, w_gate, b_gate,
                  w_out, w_ffn_in, w_ffn_out):
    poolw = jnp.zeros((256, 256), F32)
    for g in range(256 // POOL_GDIM):
        poolw = poolw.at[g * POOL_GDIM:(g + 1) * POOL_GDIM,
                         g * POOL_GDIM:(g + 1) * POOL_GDIM].set(pool_w[l, g])
    return dict(
        w_in=w_in[l].astype(BF16),
        gpre=norm_mix_pre[l].reshape(1, D_MODEL),
        gpost=norm_mix_post[l].reshape(1, D_MODEL),
        convw=conv_w[l],
        poolw=poolw.astype(BF16),
        poolscale=pool_scale[l].reshape(1, 256),
        sgnorm=sg_norm[l].reshape(1, 256),
        sgwt=jnp.swapaxes(sg_w[l], 1, 2).astype(BF16),
        sgb=jnp.repeat(sg_b[l].T, SG_GDIM, axis=1),
        sink=attn_sink[l],
        wg=w_gate[l].astype(BF16),
        bg=b_gate[l],
        wb=w_branch[l].astype(BF16),
        wo=w_out[l].reshape(N_COL_CHUNKS, MXU, D_MODEL).astype(BF16),
        ffn_gpre=norm_ffn_pre[l].reshape(1, D_MODEL),
        ffn_gpost=norm_ffn_post[l].reshape(1, D_MODEL),
        ffn_win=w_ffn_in[l].astype(BF16),
        ffn_wout=w_ffn_out[l].astype(BF16),
    )


def _trunk(x, layers):
    b, s, d = x.shape
    x = x.reshape(b * s, d)
    for p in layers:
        h, z = _in_proj(x, p["gpre"], p["w_in"])
        x = _mix(x, h, z, p, s)
        x = _ffn(x, p)
    return x.reshape(b, s, d)


def kernel(x_prompt, x_sample, norm_mix_pre, norm_mix_post, norm_ffn_pre, norm_ffn_post, w_in,
           conv_w, pool_w, pool_scale, sg_norm, sg_w, sg_b, attn_sink, w_branch, w_gate, b_gate,
           w_out, w_ffn_in, w_ffn_out):
    depth = w_in.shape[0]
    layers = [
        _layer_params(l, norm_mix_pre, norm_mix_post, norm_ffn_pre, norm_ffn_post, w_in, conv_w,
                      pool_w, pool_scale, sg_norm, sg_w, sg_b, attn_sink, w_branch, w_gate,
                      b_gate, w_out, w_ffn_in, w_ffn_out)
        for l in range(depth)]
    return (_trunk(x_prompt, layers), _trunk(x_sample, layers))
```

```python
import functools

import jax
import jax.numpy as jnp
from jax import lax
from jax.experimental import pallas as pl
from jax.experimental.pallas import tpu as pltpu

D_MODEL = 1024
EPS = 1e-6
BRANCH_WIDTH = 256
N_BRANCH = 4
POOL_GDIM = 64
CHUNK = 128
SG_GROUPS = 4
SG_GDIM = 64
N_Q_HEADS = 4
N_KV_HEADS = 2
HEAD_DIM = 64
WINDOW = 128
ATTN_BLOCK = 128
NEG_BIG = -1e30
D_IN = 2048
D_FF = 2816

COL_BGATE, COL_CGATE, COL_XIN, COL_POOL = 0, 256, 512, 768
COL_SG_U, COL_SG_V, COL_Q, COL_K, COL_V = 1024, 1280, 1536, 1792, 1920
KV_COL_BLOCK = COL_K // 256

LANES = 128
MXU = 256
HALO_ROWS = 16
N_COL_CHUNKS = D_MODEL // MXU
N_FF_CHUNKS = D_FF // MXU
TILE = 512
MIX_STEP = 1024
IN_TILE = 2048
FFN_TILE = 1024
SUB_TILE = 512
VMEM_LIMIT_BYTES = 56 * 1024 * 1024
MIX_VMEM_LIMIT_BYTES = 58 * 1024 * 1024

BF16 = jnp.bfloat16
F32 = jnp.float32


def _dot(a, b):
    return jnp.dot(a, b, preferred_element_type=F32)


def _rms_norm(x, gain):
    ms = jnp.mean(x * x, axis=-1, keepdims=True)
    return x * lax.rsqrt(ms + EPS) * gain


def _const_spec(shape):
    nd = len(shape)
    return pl.BlockSpec(shape, lambda i: (0,) * nd, pipeline_mode=pl.Buffered(1))


def _in_proj_kernel(x_ref, gain_ref, w_ref, h_ref, z_ref):
    for r in range(IN_TILE // SUB_TILE):
        rows = slice(r * SUB_TILE, (r + 1) * SUB_TILE)
        h = _rms_norm(x_ref[rows, :], gain_ref[...]).astype(BF16)
        h_ref[rows, :] = h
        z_ref[rows, :] = _dot(h, w_ref[...]).astype(BF16)


def _in_proj(x, gain, w_in):
    n = x.shape[0]
    return pl.pallas_call(
        _in_proj_kernel,
        grid=(n // IN_TILE,),
        in_specs=[
            pl.BlockSpec((IN_TILE, D_MODEL), lambda i: (i, 0)),
            _const_spec((1, D_MODEL)),
            _const_spec((D_MODEL, D_IN)),
        ],
        out_specs=[pl.BlockSpec((IN_TILE, D_MODEL), lambda i: (i, 0)),
                   pl.BlockSpec((IN_TILE, D_IN), lambda i: (i, 0))],
        out_shape=[jax.ShapeDtypeStruct((n, D_MODEL), BF16),
                   jax.ShapeDtypeStruct((n, D_IN), BF16)],
        compiler_params=pltpu.CompilerParams(
            dimension_semantics=("arbitrary",), vmem_limit_bytes=VMEM_LIMIT_BYTES),
        name="in_proj",
    )(x, gain, w_in)


def _attn_bias_t():
    key = lax.broadcasted_iota(jnp.int32, (3 * ATTN_BLOCK, N_Q_HEADS * ATTN_BLOCK), 0)
    col = lax.broadcasted_iota(jnp.int32, (3 * ATTN_BLOCK, N_Q_HEADS * ATTN_BLOCK), 1)
    head = sum((col >= hd * ATTN_BLOCK).astype(jnp.int32) for hd in range(1, N_Q_HEADS))
    rel = jnp.abs(col - head * ATTN_BLOCK - key + ATTN_BLOCK)
    slope = jnp.full(key.shape, 2.0 ** (-8.0 / N_Q_HEADS), F32)
    for hd in range(1, N_Q_HEADS):
        slope = jnp.where(head == hd, 2.0 ** (-8.0 * (hd + 1) / N_Q_HEADS), slope)
    return jnp.where(rel <= WINDOW, -slope * rel.astype(F32), NEG_BIG)


def _mix_kernel(x_ref, h_ref, z_ref, zprev_ref, znext_ref, kvprev_ref, kvnext_ref,
                convw_ref, poolw_ref, poolscale_ref, sgnorm_ref, sgwt_ref, sgb_ref,
                sink_ref, wg_ref, bg_ref, wb_ref, wo_ref, gpost_ref,
                o_ref,
                uext_ref, pext_ref, psum_ref, kvext_ref, vt_ref, qt_ref, br_ref, bias_ref,
                merged_ref,
                *, seq_len):
    step, tile = MIX_STEP, TILE
    i = pl.program_id(0)
    steps_per_seq = seq_len // step
    s_in_seq = lax.rem(i, steps_per_seq)
    is_first = s_in_seq == 0
    is_last = s_in_seq == steps_per_seq - 1
    n_blocks = step // ATTN_BLOCK

    @pl.when(i == 0)
    def _():
        bias_ref[...] = _attn_bias_t()

    def attn_prepare():
        kvext_ref[0:ATTN_BLOCK, :] = kvprev_ref[...]
        kvext_ref[ATTN_BLOCK:ATTN_BLOCK + step, :] = z_ref[:, COL_K:COL_K + 256]
        kvext_ref[ATTN_BLOCK + step:, :] = kvnext_ref[...]
        vt_ref[...] = kvext_ref[:, LANES:].astype(F32).T.astype(BF16)
        qt_ref[...] = z_ref[:, COL_Q:COL_Q + 256].astype(F32).T.astype(BF16)

    key_row = lax.broadcasted_iota(jnp.int32, (3 * ATTN_BLOCK, N_Q_HEADS * ATTN_BLOCK), 0)
    head_col = lax.broadcasted_iota(jnp.int32, (1, N_Q_HEADS * ATTN_BLOCK), 1)
    sink_row = jnp.full((1, N_Q_HEADS * ATTN_BLOCK), sink_ref[0], F32)
    for hd in range(1, N_Q_HEADS):
        sink_row = jnp.where(head_col >= hd * ATTN_BLOCK, sink_ref[hd], sink_row)

    def attn_scores(b):
        r0 = b * ATTN_BLOCK
        keys = kvext_ref[r0:r0 + 3 * ATTN_BLOCK, 0:LANES]
        qt = [qt_ref[hd * HEAD_DIM:(hd + 1) * HEAD_DIM, r0:r0 + ATTN_BLOCK]
              for hd in range(N_Q_HEADS)]
        zero = jnp.zeros((HEAD_DIM, ATTN_BLOCK), BF16)
        rhs = jnp.concatenate(
            [jnp.concatenate([qt[0], qt[1], zero, zero], axis=1),
             jnp.concatenate([zero, zero, qt[2], qt[3]], axis=1)], axis=0)
        s = _dot(keys, rhs) * (HEAD_DIM ** -0.5) + bias_ref[...]
        if b == 0:
            s = s + jnp.where(jnp.logical_and(is_first, key_row < ATTN_BLOCK), NEG_BIG, 0.0)
        if b == n_blocks - 1:
            s = s + jnp.where(jnp.logical_and(is_last, key_row >= 2 * ATTN_BLOCK), NEG_BIG, 0.0)
        return s

    def attn_values(b, s):
        r0 = b * ATTN_BLOCK
        m = jnp.maximum(jnp.max(s, axis=0, keepdims=True), sink_row)
        p = jnp.exp(s - m)
        denom = jnp.sum(p, axis=0, keepdims=True) + jnp.exp(sink_row - m)
        p = p.astype(BF16)
        for kh in range(N_KV_HEADS):
            cols = slice(kh * 2 * ATTN_BLOCK, (kh + 1) * 2 * ATTN_BLOCK)
            vt = vt_ref[kh * HEAD_DIM:(kh + 1) * HEAD_DIM, r0:r0 + 3 * ATTN_BLOCK]
            o_t = _dot(vt, p[:, cols]) / denom[:, cols]
            both = jnp.concatenate([o_t[:, :ATTN_BLOCK], o_t[:, ATTN_BLOCK:]], axis=0)
            br_ref[3, r0:r0 + ATTN_BLOCK, kh * LANES:(kh + 1) * LANES] = both.T.astype(BF16)

    def fill_with_halos(dst_ref, sub, value_of):
        split = tile + HALO_ROWS
        if sub == 0:
            dst_ref[0:HALO_ROWS, :] = jnp.where(is_first, 0.0, value_of(zprev_ref, slice(None)))
            dst_ref[HALO_ROWS:HALO_ROWS + split, :] = value_of(z_ref, slice(0, split))
        else:
            dst_ref[HALO_ROWS + split:HALO_ROWS + step, :] = value_of(z_ref, slice(split, step))
            dst_ref[HALO_ROWS + step:, :] = jnp.where(is_last, 0.0, value_of(znext_ref, slice(None)))

    def conv_fill(sub):
        def gated_u(ref, rows):
            c = ref[rows, COL_CGATE:COL_CGATE + 256].astype(F32)
            xin = ref[rows, COL_XIN:COL_XIN + 256].astype(F32)
            return c * xin

        fill_with_halos(uext_ref, sub, gated_u)

    def conv_rows(sub, half):
        n = tile // 2
        r0 = sub * tile + half * n
        conv = (convw_ref[0:1, :] * uext_ref[HALO_ROWS - 1 + r0:HALO_ROWS - 1 + r0 + n, :]
                + convw_ref[1:2, :] * uext_ref[HALO_ROWS + r0:HALO_ROWS + r0 + n, :]
                + convw_ref[2:3, :] * uext_ref[HALO_ROWS + 1 + r0:HALO_ROWS + 1 + r0 + n, :])
        bgate = z_ref[r0:r0 + n, COL_BGATE:COL_BGATE + 256].astype(F32)
        br_ref[0, r0:r0 + n, :] = (bgate * conv).astype(BF16)

    def pool_fill(sub):
        fill_with_halos(pext_ref, sub,
                        lambda ref, rows: ref[rows, COL_POOL:COL_POOL + 256].astype(F32))

    def pool_sums_wide(sub, half):
        lo, hi = (0, tile + HALO_ROWS) if sub == 0 else (tile + HALO_ROWS, step + HALO_ROWS)
        rows = (hi - lo) // 2
        r0 = lo + half * rows
        acc = pext_ref[HALO_ROWS - 8 + r0:HALO_ROWS - 8 + r0 + rows, LANES:]
        for back in range(1, 8):
            acc = acc + pext_ref[HALO_ROWS - back + r0:HALO_ROWS - back + r0 + rows, LANES:]
        psum_ref[r0:r0 + rows, :] = acc

    def inv_count(w, row0, rows):
        pos = s_in_seq * step + row0 + lax.broadcasted_iota(jnp.int32, (rows, LANES), 0)
        hi = jnp.minimum(pos + w // 2, seq_len)
        lo = jnp.maximum(pos - w // 2, 0)
        return 1.0 / (hi - lo).astype(F32)

    def pool_branch(sub, half):
        def shifted(off, rows):
            return pext_ref[HALO_ROWS + off + rows.start:HALO_ROWS + off + rows.stop, 0:LANES]

        def pooled_rows(rows, exact_counts):
            n = rows.stop - rows.start
            lane = lax.broadcasted_iota(jnp.int32, (n, LANES), 1)
            s2 = shifted(-1, rows) + shifted(0, rows)
            s4 = s2 + shifted(-2, rows) + shifted(1, rows)
            s8 = psum_ref[rows.start + 4:rows.stop + 4, :]
            s16 = psum_ref[rows.start:rows.stop, :] + psum_ref[rows.start + 8:rows.stop + 8, :]
            if exact_counts:
                inv = [inv_count(w, rows.start, n) for w in (2, 4, 8, 16)]
            else:
                inv = [1.0 / w for w in (2, 4, 8, 16)]
            mean = jnp.concatenate(
                [jnp.where(lane < POOL_GDIM, s2 * inv[0], s4 * inv[1]),
                 jnp.where(lane < POOL_GDIM, s8 * inv[2], s16 * inv[3])], axis=1)
            return mean - pext_ref[HALO_ROWS + rows.start:HALO_ROWS + rows.stop, :]

        r0 = sub * tile + half * (tile // 2)
        r1 = r0 + tile // 2
        if r0 == 0:
            pooled = jnp.concatenate(
                [pooled_rows(slice(0, HALO_ROWS), True),
                 pooled_rows(slice(HALO_ROWS, r1), False)], axis=0)
        elif r1 == step:
            pooled = jnp.concatenate(
                [pooled_rows(slice(r0, step - HALO_ROWS), False),
                 pooled_rows(slice(step - HALO_ROWS, step), True)], axis=0)
        else:
            pooled = pooled_rows(slice(r0, r1), False)
        br_ref[1, r0:r1, :] = (
            _dot(pooled.astype(BF16), poolw_ref[...]) * poolscale_ref[...]).astype(BF16)

    def gating_norm(sub):
        v = z_ref[sub * tile:(sub + 1) * tile, COL_SG_V:COL_SG_V + 256].astype(F32)
        mu = jnp.mean(v, axis=-1, keepdims=True)
        vc = v - mu
        var = jnp.mean(vc * vc, axis=-1, keepdims=True)
        return (vc * lax.rsqrt(var + EPS) * sgnorm_ref[...]).T

    def gating_mix(sub, vln_t):
        n_chunks = tile // CHUNK
        mixed_t = []
        for g in range(SG_GROUPS):
            lhs = jnp.concatenate(
                [vln_t[g * SG_GDIM:(g + 1) * SG_GDIM, n * CHUNK:(n + 1) * CHUNK]
                 for n in range(n_chunks)], axis=0)
            mixed_t.append(_dot(lhs.astype(BF16), sgwt_ref[g]))
        for n in range(n_chunks):
            rows = slice(sub * tile + n * CHUNK, sub * tile + (n + 1) * CHUNK)
            tiles = []
            for j in range(2):
                piece = jnp.concatenate(
                    [mixed_t[2 * j][n * SG_GDIM:(n + 1) * SG_GDIM],
                     mixed_t[2 * j + 1][n * SG_GDIM:(n + 1) * SG_GDIM]], axis=0)
                tiles.append(piece.T)
            mixed = jnp.concatenate(tiles, axis=1) + sgb_ref[...]
            u = z_ref[rows, COL_SG_U:COL_SG_U + 256].astype(F32)
            br_ref[2, rows, :] = (u * mixed).astype(BF16)

    def gate(b, c, h):
        cols = slice(c * MXU, (c + 1) * MXU)
        return jax.nn.sigmoid(_dot(h, wg_ref[b, :, cols]) + bg_ref[b:b + 1, cols])

    def project(sub, b, gates):
        for c in range(N_COL_CHUNKS):
            cols = slice(c * MXU, (c + 1) * MXU)
            term = gates[c] * _dot(br_ref[b, sub * tile:(sub + 1) * tile, :], wb_ref[b, :, cols])
            if b == 0:
                merged_ref[:, cols] = term
            else:
                merged_ref[:, cols] += term

    def finish(sub, half):
        n = tile // 2
        y = jnp.zeros((n, D_MODEL), F32)
        for c in range(N_COL_CHUNKS):
            y = y + _dot(merged_ref[half * n:(half + 1) * n, c * MXU:(c + 1) * MXU].astype(BF16),
                         wo_ref[c])
        rows = slice(sub * tile + half * n, sub * tile + (half + 1) * n)
        o_ref[rows, :] = x_ref[rows, :] + _rms_norm(y, gpost_ref[...])

    def mix_sub_tile(sub):
        blk = sub * (tile // ATTN_BLOCK)
        sc0 = attn_scores(blk)
        sc1 = attn_scores(blk + 1)
        h = h_ref[sub * tile:(sub + 1) * tile, :]

        g = [gate(0, 0, h)]
        conv_fill(sub)
        g.append(gate(0, 1, h))
        conv_rows(sub, 0)
        g.append(gate(0, 2, h))
        conv_rows(sub, 1)
        g.append(gate(0, 3, h))
        attn_values(blk, sc0)
        project(sub, 0, g)

        g = [gate(2, 0, h)]
        vln_t = gating_norm(sub)
        g.append(gate(2, 1, h))
        gating_mix(sub, vln_t)
        g.append(gate(2, 2, h))
        pool_fill(sub)
        g.append(gate(2, 3, h))
        attn_values(blk + 1, sc1)
        project(sub, 2, g)

        g = [gate(1, 0, h)]
        pool_sums_wide(sub, 0)
        pool_sums_wide(sub, 1)
        g.append(gate(1, 1, h))
        pool_branch(sub, 0)
        sc2 = attn_scores(blk + 2)
        g.append(gate(1, 2, h))
        pool_branch(sub, 1)
        g.append(gate(1, 3, h))
        attn_values(blk + 2, sc2)
        project(sub, 1, g)

        g = [gate(3, 0, h)]
        sc3 = attn_scores(blk + 3)
        g.append(gate(3, 1, h))
        attn_values(blk + 3, sc3)
        g.append(gate(3, 2, h))
        g.append(gate(3, 3, h))
        project(sub, 3, g)

        finish(sub, 0)
        finish(sub, 1)

    attn_prepare()
    for sub in range(step // tile):
        mix_sub_tile(sub)


def _mix(x, h, z, p, seq_len):
    n = x.shape[0]
    step = MIX_STEP
    assert TILE == N_BRANCH * ATTN_BLOCK and step == 2 * TILE and seq_len % step == 0
    r16 = step // HALO_ROWS
    r128 = step // ATTN_BLOCK
    last16 = n // HALO_ROWS - 1
    last128 = n // ATTN_BLOCK - 1
    kernel = functools.partial(_mix_kernel, seq_len=seq_len)
    return pl.pallas_call(
        kernel,
        grid=(n // step,),
        in_specs=[
            pl.BlockSpec((step, D_MODEL), lambda i: (i, 0)),
            pl.BlockSpec((step, D_MODEL), lambda i: (i, 0)),
            pl.BlockSpec((step, D_IN), lambda i: (i, 0)),
            pl.BlockSpec((HALO_ROWS, 1024), lambda i: (jnp.maximum(i * r16 - 1, 0), 0)),
            pl.BlockSpec((HALO_ROWS, 1024), lambda i: (jnp.minimum((i + 1) * r16, last16), 0)),
            pl.BlockSpec((ATTN_BLOCK, 256),
                         lambda i: (jnp.maximum(i * r128 - 1, 0), KV_COL_BLOCK)),
            pl.BlockSpec((ATTN_BLOCK, 256),
                         lambda i: (jnp.minimum((i + 1) * r128, last128), KV_COL_BLOCK)),
            _const_spec((3, 256)),
            _const_spec((256, 256)),
            _const_spec((1, 256)),
            _const_spec((1, 256)),
            _const_spec((SG_GROUPS, CHUNK, CHUNK)),
            _const_spec((CHUNK, 256)),
            pl.BlockSpec(memory_space=pltpu.SMEM),
            _const_spec((N_BRANCH, D_MODEL, D_MODEL)),
            _const_spec((N_BRANCH, D_MODEL)),
            _const_spec((N_BRANCH, BRANCH_WIDTH, D_MODEL)),
            _const_spec((N_COL_CHUNKS, MXU, D_MODEL)),
            _const_spec((1, D_MODEL)),
        ],
        out_specs=pl.BlockSpec((step, D_MODEL), lambda i: (i, 0)),
        out_shape=jax.ShapeDtypeStruct((n, D_MODEL), F32),
        scratch_shapes=[
            pltpu.VMEM((step + 2 * HALO_ROWS, 256), F32),
            pltpu.VMEM((step + 2 * HALO_ROWS, 256), F32),
            pltpu.VMEM((step + HALO_ROWS, LANES), F32),
            pltpu.VMEM((step + 2 * ATTN_BLOCK, 256), BF16),
            pltpu.VMEM((LANES, step + 2 * ATTN_BLOCK), BF16),
            pltpu.VMEM((N_Q_HEADS * HEAD_DIM, step), BF16),
            pltpu.VMEM((N_BRANCH, step, BRANCH_WIDTH), BF16),
            pltpu.VMEM((3 * ATTN_BLOCK, N_Q_HEADS * ATTN_BLOCK), F32),
            pltpu.VMEM((TILE, D_MODEL), F32),
        ],
        compiler_params=pltpu.CompilerParams(
            dimension_semantics=("arbitrary",), vmem_limit_bytes=MIX_VMEM_LIMIT_BYTES),
        name="mix",
    )(x, h, z, z, z, z, z,
      p["convw"], p["poolw"], p["poolscale"], p["sgnorm"], p["sgwt"], p["sgb"],
      p["sink"], p["wg"], p["bg"], p["wb"], p["wo"], p["gpost"])


def _ffn_kernel(x_ref, gpre_ref, win_ref, wout_ref, gpost_ref, o_ref, act_ref):
    for r in range(FFN_TILE // SUB_TILE):
        rows = slice(r * SUB_TILE, (r + 1) * SUB_TILE)
        x = x_ref[rows, :]
        h = _rms_norm(x, gpre_ref[...]).astype(BF16)
        for c in range(N_FF_CHUNKS):
            a = _dot(h, win_ref[:, c * MXU:(c + 1) * MXU])
            g = _dot(h, win_ref[:, D_FF + c * MXU:D_FF + (c + 1) * MXU])
            act_ref[r, :, c * MXU:(c + 1) * MXU] = (a * jax.nn.sigmoid(a) * g).astype(BF16)
        y = _dot(act_ref[r], wout_ref[...])
        o_ref[rows, :] = x + _rms_norm(y, gpost_ref[...])


def _ffn(x, p):
    n = x.shape[0]
    return pl.pallas_call(
        _ffn_kernel,
        grid=(n // FFN_TILE,),
        in_specs=[
            pl.BlockSpec((FFN_TILE, D_MODEL), lambda i: (i, 0)),
            _const_spec((1, D_MODEL)),
            _const_spec((D_MODEL, 2 * D_FF)),
            _const_spec((D_FF, D_MODEL)),
            _const_spec((1, D_MODEL)),
        ],
        out_specs=pl.BlockSpec((FFN_TILE, D_MODEL), lambda i: (i, 0)),
        out_shape=jax.ShapeDtypeStruct((n, D_MODEL), F32),
        scratch_shapes=[pltpu.VMEM((FFN_TILE // SUB_TILE, SUB_TILE, D_FF), BF16)],
        compiler_params=pltpu.CompilerParams(
            dimension_semantics=("arbitrary",), vmem_limit_bytes=VMEM_LIMIT_BYTES),
        name="ffn",
    )(x, p["ffn_gpre"], p["ffn_win"], p["ffn_wout"], p["ffn_gpost"])


def _layer_params(l, norm_mix_pre, norm_mix_post, norm_ffn_pre, norm_ffn_post, w_in, conv_w,
                  pool_w, pool_scale, sg_norm, sg_w, sg_b, attn_sink, w_branch, w_gate, b_gate,
                  w_out, w_ffn_in, w_ffn_out):
    poolw = jnp.zeros((256, 256), F32)
    for g in range(256 // POOL_GDIM):
        poolw = poolw.at[g * POOL_GDIM:(g + 1) * POOL_GDIM,
                         g * POOL_GDIM:(g + 1) * POOL_GDIM].set(pool_w[l, g])
    return dict(
        w_in=w_in[l].astype(BF16),
        gpre=norm_mix_pre[l].reshape(1, D_MODEL),
        gpost=norm_mix_post[l].reshape(1, D_MODEL),
        convw=conv_w[l],
        poolw=poolw.astype(BF16),
        poolscale=pool_scale[l].reshape(1, 256),
        sgnorm=sg_norm[l].reshape(1, 256),
        sgwt=jnp.swapaxes(sg_w[l], 1, 2).astype(BF16),
        sgb=jnp.repeat(sg_b[l].T, SG_GDIM, axis=1),
        sink=attn_sink[l],
        wg=w_gate[l].astype(BF16),
        bg=b_gate[l],
        wb=w_branch[l].astype(BF16),
        wo=w_out[l].reshape(N_COL_CHUNKS, MXU, D_MODEL).astype(BF16),
        ffn_gpre=norm_ffn_pre[l].reshape(1, D_MODEL),
        ffn_gpost=norm_ffn_post[l].reshape(1, D_MODEL),
        ffn_win=w_ffn_in[l].astype(BF16),
        ffn_wout=w_ffn_out[l].astype(BF16),
    )


def _trunk(x, layers):
    b, s, d = x.shape
    x = x.reshape(b * s, d)
    for p in layers:
        h, z = _in_proj(x, p["gpre"], p["w_in"])
        x = _mix(x, h, z, p, s)
        x = _ffn(x, p)
    return x.reshape(b, s, d)


def kernel(x_prompt, x_sample, norm_mix_pre, norm_mix_post, norm_ffn_pre, norm_ffn_post, w_in,
           conv_w, pool_w, pool_scale, sg_norm, sg_w, sg_b, attn_sink, w_branch, w_gate, b_gate,
           w_out, w_ffn_in, w_ffn_out):
    depth = w_in.shape[0]
    layers = [
        _layer_params(l, norm_mix_pre, norm_mix_post, norm_ffn_pre, norm_ffn_post, w_in, conv_w,
                      pool_w, pool_scale, sg_norm, sg_w, sg_b, attn_sink, w_branch, w_gate,
                      b_gate, w_out, w_ffn_in, w_ffn_out)
        for l in range(depth)]
    return (_trunk(x_prompt, layers), _trunk(x_sample, layers))
```

```python
import functools

import jax
import jax.numpy as jnp
from jax import lax
from jax.experimental import pallas as pl
from jax.experimental.pallas import tpu as pltpu

D_MODEL = 1024
EPS = 1e-6
BRANCH_WIDTH = 256
N_BRANCH = 4
POOL_GDIM = 64
CHUNK = 128
SG_GROUPS = 4
SG_GDIM = 64
N_Q_HEADS = 4
N_KV_HEADS = 2
HEAD_DIM = 64
WINDOW = 128
ATTN_BLOCK = 128
NEG_BIG = -1e30
D_IN = 2048
D_FF = 2816

COL_BGATE, COL_CGATE, COL_XIN, COL_POOL = 0, 256, 512, 768
COL_SG_U, COL_SG_V, COL_Q, COL_K, COL_V = 1024, 1280, 1536, 1792, 1920
KV_COL_BLOCK = COL_K // 256

LANES = 128
MXU = 256
HALO_ROWS = 16
N_COL_CHUNKS = D_MODEL // MXU
N_FF_CHUNKS = D_FF // MXU
TILE = 512
MIX_STEP = 1024
IN_TILE = 2048
FFN_TILE = 1024
SUB_TILE = 512
VMEM_LIMIT_BYTES = 56 * 1024 * 1024
MIX_VMEM_LIMIT_BYTES = 58 * 1024 * 1024

BF16 = jnp.bfloat16
F32 = jnp.float32


def _dot(a, b):
    return jnp.dot(a, b, preferred_element_type=F32)


def _rms_norm(x, gain):
    ms = jnp.mean(x * x, axis=-1, keepdims=True)
    return x * lax.rsqrt(ms + EPS) * gain


def _const_spec(shape):
    nd = len(shape)
    return pl.BlockSpec(shape, lambda i: (0,) * nd, pipeline_mode=pl.Buffered(1))


def _in_proj_kernel(x_ref, gain_ref, w_ref, h_ref, z_ref):
    for r in range(IN_TILE // SUB_TILE):
        rows = slice(r * SUB_TILE, (r + 1) * SUB_TILE)
        h = _rms_norm(x_ref[rows, :], gain_ref[...]).astype(BF16)
        h_ref[rows, :] = h
        z_ref[rows, :] = _dot(h, w_ref[...]).astype(BF16)


def _in_proj(x, gain, w_in):
    n = x.shape[0]
    return pl.pallas_call(
        _in_proj_kernel,
        grid=(n // IN_TILE,),
        in_specs=[
            pl.BlockSpec((IN_TILE, D_MODEL), lambda i: (i, 0)),
            _const_spec((1, D_MODEL)),
            _const_spec((D_MODEL, D_IN)),
        ],
        out_specs=[pl.BlockSpec((IN_TILE, D_MODEL), lambda i: (i, 0)),
                   pl.BlockSpec((IN_TILE, D_IN), lambda i: (i, 0))],
        out_shape=[jax.ShapeDtypeStruct((n, D_MODEL), BF16),
                   jax.ShapeDtypeStruct((n, D_IN), BF16)],
        compiler_params=pltpu.CompilerParams(
            dimension_semantics=("arbitrary",), vmem_limit_bytes=VMEM_LIMIT_BYTES),
        name="in_proj",
    )(x, gain, w_in)


def _attn_bias_t():
    key = lax.broadcasted_iota(jnp.int32, (3 * ATTN_BLOCK, N_Q_HEADS * ATTN_BLOCK), 0)
    col = lax.broadcasted_iota(jnp.int32, (3 * ATTN_BLOCK, N_Q_HEADS * ATTN_BLOCK), 1)
    head = sum((col >= hd * ATTN_BLOCK).astype(jnp.int32) for hd in range(1, N_Q_HEADS))
    rel = jnp.abs(col - head * ATTN_BLOCK - key + ATTN_BLOCK)
    slope = jnp.full(key.shape, 2.0 ** (-8.0 / N_Q_HEADS), F32)
    for hd in range(1, N_Q_HEADS):
        slope = jnp.where(head == hd, 2.0 ** (-8.0 * (hd + 1) / N_Q_HEADS), slope)
    return jnp.where(rel <= WINDOW, -slope * rel.astype(F32), NEG_BIG)


def _mix_kernel(x_ref, h_ref, z_ref, zprev_ref, znext_ref, kvprev_ref, kvnext_ref,
                convw_ref, poolw_ref, poolscale_ref, sgnorm_ref, sgwt_ref, sgb_ref,
                sink_ref, wg_ref, bg_ref, wb_ref, wo_ref, gpost_ref,
                o_ref,
                uext_ref, pext_ref, psum_ref, kvext_ref, vt_ref, qt_ref, br_ref, bias_ref,
                merged_ref,
                *, seq_len):
    step, tile = MIX_STEP, TILE
    i = pl.program_id(0)
    steps_per_seq = seq_len // step
    s_in_seq = lax.rem(i, steps_per_seq)
    is_first = s_in_seq == 0
    is_last = s_in_seq == steps_per_seq - 1
    n_blocks = step // ATTN_BLOCK

    @pl.when(i == 0)
    def _():
        bias_ref[...] = _attn_bias_t()

    def attn_prepare():
        kvext_ref[0:ATTN_BLOCK, :] = kvprev_ref[...]
        kvext_ref[ATTN_BLOCK:ATTN_BLOCK + step, :] = z_ref[:, COL_K:COL_K + 256]
        kvext_ref[ATTN_BLOCK + step:, :] = kvnext_ref[...]
        vt_ref[...] = kvext_ref[:, LANES:].astype(F32).T.astype(BF16)
        qt_ref[...] = z_ref[:, COL_Q:COL_Q + 256].astype(F32).T.astype(BF16)

    key_row = lax.broadcasted_iota(jnp.int32, (3 * ATTN_BLOCK, N_Q_HEADS * ATTN_BLOCK), 0)
    head_col = lax.broadcasted_iota(jnp.int32, (1, N_Q_HEADS * ATTN_BLOCK), 1)
    sink_row = jnp.full((1, N_Q_HEADS * ATTN_BLOCK), sink_ref[0], F32)
    for hd in range(1, N_Q_HEADS):
        sink_row = jnp.where(head_col >= hd * ATTN_BLOCK, sink_ref[hd], sink_row)

    def attn_scores(b):
        r0 = b * ATTN_BLOCK
        keys = kvext_ref[r0:r0 + 3 * ATTN_BLOCK, 0:LANES]
        qt = [qt_ref[hd * HEAD_DIM:(hd + 1) * HEAD_DIM, r0:r0 + ATTN_BLOCK]
              for hd in range(N_Q_HEADS)]
        zero = jnp.zeros((HEAD_DIM, ATTN_BLOCK), BF16)
        rhs = jnp.concatenate(
            [jnp.concatenate([qt[0], qt[1], zero, zero], axis=1),
             jnp.concatenate([zero, zero, qt[2], qt[3]], axis=1)], axis=0)
        s = _dot(keys, rhs) * (HEAD_DIM ** -0.5) + bias_ref[...]
        if b == 0:
            s = s + jnp.where(jnp.logical_and(is_first, key_row < ATTN_BLOCK), NEG_BIG, 0.0)
        if b == n_blocks - 1:
            s = s + jnp.where(jnp.logical_and(is_last, key_row >= 2 * ATTN_BLOCK), NEG_BIG, 0.0)
        return s

    def attn_values(b, s):
        r0 = b * ATTN_BLOCK
        m = jnp.maximum(jnp.max(s, axis=0, keepdims=True), sink_row)
        p = jnp.exp(s - m)
        denom = jnp.sum(p, axis=0, keepdims=True) + jnp.exp(sink_row - m)
        p = p.astype(BF16)
        for kh in range(N_KV_HEADS):
            cols = slice(kh * 2 * ATTN_BLOCK, (kh + 1) * 2 * ATTN_BLOCK)
            vt = vt_ref[kh * HEAD_DIM:(kh + 1) * HEAD_DIM, r0:r0 + 3 * ATTN_BLOCK]
            o_t = _dot(vt, p[:, cols]) / denom[:, cols]
            both = jnp.concatenate([o_t[:, :ATTN_BLOCK], o_t[:, ATTN_BLOCK:]], axis=0)
            br_ref[3, r0:r0 + ATTN_BLOCK, kh * LANES:(kh + 1) * LANES] = both.T.astype(BF16)

    def fill_with_halos(dst_ref, sub, value_of):
        split = tile + HALO_ROWS
        if sub == 0:
            dst_ref[0:HALO_ROWS, :] = jnp.where(is_first, 0.0, value_of(zprev_ref, slice(None)))
            dst_ref[HALO_ROWS:HALO_ROWS + split, :] = value_of(z_ref, slice(0, split))
        else:
            dst_ref[HALO_ROWS + split:HALO_ROWS + step, :] = value_of(z_ref, slice(split, step))
            dst_ref[HALO_ROWS + step:, :] = jnp.where(is_last, 0.0, value_of(znext_ref, slice(None)))

    def conv_fill(sub):
        def gated_u(ref, rows):
            c = ref[rows, COL_CGATE:COL_CGATE + 256].astype(F32)
            xin = ref[rows, COL_XIN:COL_XIN + 256].astype(F32)
            return c * xin

        fill_with_halos(uext_ref, sub, gated_u)

    def conv_rows(sub, half):
        n = tile // 2
        r0 = sub * tile + half * n
        conv = (convw_ref[0:1, :] * uext_ref[HALO_ROWS - 1 + r0:HALO_ROWS - 1 + r0 + n, :]
                + convw_ref[1:2, :] * uext_ref[HALO_ROWS + r0:HALO_ROWS + r0 + n, :]
                + convw_ref[2:3, :] * uext_ref[HALO_ROWS + 1 + r0:HALO_ROWS + 1 + r0 + n, :])
        bgate = z_ref[r0:r0 + n, COL_BGATE:COL_BGATE + 256].astype(F32)
        br_ref[0, r0:r0 + n, :] = (bgate * conv).astype(BF16)

    def pool_fill(sub):
        fill_with_halos(pext_ref, sub,
                        lambda ref, rows: ref[rows, COL_POOL:COL_POOL + 256].astype(F32))

    def pool_sums_wide(sub, half):
        lo, hi = (0, tile + HALO_ROWS) if sub == 0 else (tile + HALO_ROWS, step + HALO_ROWS)
        rows = (hi - lo) // 2
        r0 = lo + half * rows
        acc = pext_ref[HALO_ROWS - 8 + r0:HALO_ROWS - 8 + r0 + rows, LANES:]
        for back in range(1, 8):
            acc = acc + pext_ref[HALO_ROWS - back + r0:HALO_ROWS - back + r0 + rows, LANES:]
        psum_ref[r0:r0 + rows, :] = acc

    def inv_count(w, row0, rows):
        pos = s_in_seq * step + row0 + lax.broadcasted_iota(jnp.int32, (rows, LANES), 0)
        hi = jnp.minimum(pos + w // 2, seq_len)
        lo = jnp.maximum(pos - w // 2, 0)
        return 1.0 / (hi - lo).astype(F32)

    def pool_branch(sub, half):
        def shifted(off, rows):
            return pext_ref[HALO_ROWS + off + rows.start:HALO_ROWS + off + rows.stop, 0:LANES]

        def pooled_rows(rows, exact_counts):
            n = rows.stop - rows.start
            lane = lax.broadcasted_iota(jnp.int32, (n, LANES), 1)
            s2 = shifted(-1, rows) + shifted(0, rows)
            s4 = s2 + shifted(-2, rows) + shifted(1, rows)
            s8 = psum_ref[rows.start + 4:rows.stop + 4, :]
            s16 = psum_ref[rows.start:rows.stop, :] + psum_ref[rows.start + 8:rows.stop + 8, :]
            if exact_counts:
                inv = [inv_count(w, rows.start, n) for w in (2, 4, 8, 16)]
            else:
                inv = [1.0 / w for w in (2, 4, 8, 16)]
            mean = jnp.concatenate(
                [jnp.where(lane < POOL_GDIM, s2 * inv[0], s4 * inv[1]),
                 jnp.where(lane < POOL_GDIM, s8 * inv[2], s16 * inv[3])], axis=1)
            return mean - pext_ref[HALO_ROWS + rows.start:HALO_ROWS + rows.stop, :]

        r0 = sub * tile + half * (tile // 2)
        r1 = r0 + tile // 2
        if r0 == 0:
            pooled = jnp.concatenate(
                [pooled_rows(slice(0, HALO_ROWS), True),
                 pooled_rows(slice(HALO_ROWS, r1), False)], axis=0)
        elif r1 == step:
            pooled = jnp.concatenate(
                [pooled_rows(slice(r0, step - HALO_ROWS), False),
                 pooled_rows(slice(step - HALO_ROWS, step), True)], axis=0)
        else:
            pooled = pooled_rows(slice(r0, r1), False)
        br_ref[1, r0:r1, :] = (
            _dot(pooled.astype(BF16), poolw_ref[...]) * poolscale_ref[...]).astype(BF16)

    def gating_norm(sub):
        v = z_ref[sub * tile:(sub + 1) * tile, COL_SG_V:COL_SG_V + 256].astype(F32)
        mu = jnp.mean(v, axis=-1, keepdims=True)
        vc = v - mu
        var = jnp.mean(vc * vc, axis=-1, keepdims=True)
        return (vc * lax.rsqrt(var + EPS) * sgnorm_ref[...]).T

    def gating_mix(sub, vln_t):
        n_chunks = tile // CHUNK
        mixed_t = []
        for g in range(SG_GROUPS):
            lhs = jnp.concatenate(
                [vln_t[g * SG_GDIM:(g + 1) * SG_GDIM, n * CHUNK:(n + 1) * CHUNK]
                 for n in range(n_chunks)], axis=0)
            mixed_t.append(_dot(lhs.astype(BF16), sgwt_ref[g]))
        for n in range(n_chunks):
            rows = slice(sub * tile + n * CHUNK, sub * tile + (n + 1) * CHUNK)
            tiles = []
            for j in range(2):
                piece = jnp.concatenate(
                    [mixed_t[2 * j][n * SG_GDIM:(n + 1) * SG_GDIM],
                     mixed_t[2 * j + 1][n * SG_GDIM:(n + 1) * SG_GDIM]], axis=0)
                tiles.append(piece.T)
            mixed = jnp.concatenate(tiles, axis=1) + sgb_ref[...]
            u = z_ref[rows, COL_SG_U:COL_SG_U + 256].astype(F32)
            br_ref[2, rows, :] = (u * mixed).astype(BF16)

    def gated_project(sub, b, c, h):
        cols = slice(c * MXU, (c + 1) * MXU)
        gate = jax.nn.sigmoid(_dot(h, wg_ref[b, :, cols]) + bg_ref[b:b + 1, cols])
        term = gate * _dot(br_ref[b, sub * tile:(sub + 1) * tile, :], wb_ref[b, :, cols])
        if b == 0:
            merged_ref[:, cols] = term
        else:
            merged_ref[:, cols] += term

    def finish(sub, half):
        n = tile // 2
        y = jnp.zeros((n, D_MODEL), F32)
        for c in range(N_COL_CHUNKS):
            y = y + _dot(merged_ref[half * n:(half + 1) * n, c * MXU:(c + 1) * MXU].astype(BF16),
                         wo_ref[c])
        rows = slice(sub * tile + half * n, sub * tile + (half + 1) * n)
        o_ref[rows, :] = x_ref[rows, :] + _rms_norm(y, gpost_ref[...])

    def mix_sub_tile(sub):
        blk = sub * (tile // ATTN_BLOCK)
        sc0 = attn_scores(blk)
        sc1 = attn_scores(blk + 1)
        h = h_ref[sub * tile:(sub + 1) * tile, :]
        conv_fill(sub)
        conv_rows(sub, 0)
        conv_rows(sub, 1)

        gated_project(sub, 0, 0, h)
        vln_t = gating_norm(sub)
        gated_project(sub, 0, 1, h)
        gating_mix(sub, vln_t)
        gated_project(sub, 0, 2, h)
        attn_values(blk, sc0)
        gated_project(sub, 0, 3, h)

        gated_project(sub, 2, 0, h)
        pool_fill(sub)
        pool_sums_wide(sub, 0)
        pool_sums_wide(sub, 1)
        gated_project(sub, 2, 1, h)
        pool_branch(sub, 0)
        gated_project(sub, 2, 2, h)
        pool_branch(sub, 1)
        attn_values(blk + 1, sc1)
        gated_project(sub, 2, 3, h)

        gated_project(sub, 1, 0, h)
        sc2 = attn_scores(blk + 2)
        gated_project(sub, 1, 1, h)
        attn_values(blk + 2, sc2)
        sc3 = attn_scores(blk + 3)
        gated_project(sub, 1, 2, h)
        attn_values(blk + 3, sc3)
        gated_project(sub, 1, 3, h)

        for c in range(N_COL_CHUNKS):
            gated_project(sub, 3, c, h)

        finish(sub, 0)
        finish(sub, 1)

    attn_prepare()
    for sub in range(step // tile):
        mix_sub_tile(sub)


def _mix(x, h, z, p, seq_len):
    n = x.shape[0]
    step = MIX_STEP
    assert TILE == N_BRANCH * ATTN_BLOCK and step == 2 * TILE and seq_len % step == 0
    r16 = step // HALO_ROWS
    r128 = step // ATTN_BLOCK
    last16 = n // HALO_ROWS - 1
    last128 = n // ATTN_BLOCK - 1
    kernel = functools.partial(_mix_kernel, seq_len=seq_len)
    return pl.pallas_call(
        kernel,
        grid=(n // step,),
        in_specs=[
            pl.BlockSpec((step, D_MODEL), lambda i: (i, 0)),
            pl.BlockSpec((step, D_MODEL), lambda i: (i, 0)),
            pl.BlockSpec((step, D_IN), lambda i: (i, 0)),
            pl.BlockSpec((HALO_ROWS, 1024), lambda i: (jnp.maximum(i * r16 - 1, 0), 0)),
            pl.BlockSpec((HALO_ROWS, 1024), lambda i: (jnp.minimum((i + 1) * r16, last16), 0)),
            pl.BlockSpec((ATTN_BLOCK, 256),
                         lambda i: (jnp.maximum(i * r128 - 1, 0), KV_COL_BLOCK)),
            pl.BlockSpec((ATTN_BLOCK, 256),
                         lambda i: (jnp.minimum((i + 1) * r128, last128), KV_COL_BLOCK)),
            _const_spec((3, 256)),
            _const_spec((256, 256)),
            _const_spec((1, 256)),
            _const_spec((1, 256)),
            _const_spec((SG_GROUPS, CHUNK, CHUNK)),
            _const_spec((CHUNK, 256)),
            pl.BlockSpec(memory_space=pltpu.SMEM),
            _const_spec((N_BRANCH, D_MODEL, D_MODEL)),
            _const_spec((N_BRANCH, D_MODEL)),
            _const_spec((N_BRANCH, BRANCH_WIDTH, D_MODEL)),
            _const_spec((N_COL_CHUNKS, MXU, D_MODEL)),
            _const_spec((1, D_MODEL)),
        ],
        out_specs=pl.BlockSpec((step, D_MODEL), lambda i: (i, 0)),
        out_shape=jax.ShapeDtypeStruct((n, D_MODEL), F32),
        scratch_shapes=[
            pltpu.VMEM((step + 2 * HALO_ROWS, 256), F32),
            pltpu.VMEM((step + 2 * HALO_ROWS, 256), F32),
            pltpu.VMEM((step + HALO_ROWS, LANES), F32),
            pltpu.VMEM((step + 2 * ATTN_BLOCK, 256), BF16),
            pltpu.VMEM((LANES, step + 2 * ATTN_BLOCK), BF16),
            pltpu.VMEM((N_Q_HEADS * HEAD_DIM, step), BF16),
            pltpu.VMEM((N_BRANCH, step, BRANCH_WIDTH), BF16),
            pltpu.VMEM((3 * ATTN_BLOCK, N_Q_HEADS * ATTN_BLOCK), F32),
            pltpu.VMEM((TILE, D_MODEL), F32),
        ],
        compiler_params=pltpu.CompilerParams(
            dimension_semantics=("arbitrary",), vmem_limit_bytes=MIX_VMEM_LIMIT_BYTES),
        name="mix",
    )(x, h, z, z, z, z, z,
      p["convw"], p["poolw"], p["poolscale"], p["sgnorm"], p["sgwt"], p["sgb"],
      p["sink"], p["wg"], p["bg"], p["wb"], p["wo"], p["gpost"])


def _ffn_kernel(x_ref, gpre_ref, win_ref, wout_ref, gpost_ref, o_ref, act_ref):
    for r in range(FFN_TILE // SUB_TILE):
        rows = slice(r * SUB_TILE, (r + 1) * SUB_TILE)
        x = x_ref[rows, :]
        h = _rms_norm(x, gpre_ref[...]).astype(BF16)
        for c in range(N_FF_CHUNKS):
            a = _dot(h, win_ref[:, c * MXU:(c + 1) * MXU])
            g = _dot(h, win_ref[:, D_FF + c * MXU:D_FF + (c + 1) * MXU])
            act_ref[r, :, c * MXU:(c + 1) * MXU] = (a * jax.nn.sigmoid(a) * g).astype(BF16)
        y = _dot(act_ref[r], wout_ref[...])
        o_ref[rows, :] = x + _rms_norm(y, gpost_ref[...])


def _ffn(x, p):
    n = x.shape[0]
    return pl.pallas_call(
        _ffn_kernel,
        grid=(n // FFN_TILE,),
        in_specs=[
            pl.BlockSpec((FFN_TILE, D_MODEL), lambda i: (i, 0)),
            _const_spec((1, D_MODEL)),
            _const_spec((D_MODEL, 2 * D_FF)),
            _const_spec((D_FF, D_MODEL)),
            _const_spec((1, D_MODEL)),
        ],
        out_specs=pl.BlockSpec((FFN_TILE, D_MODEL), lambda i: (i, 0)),
        out_shape=jax.ShapeDtypeStruct((n, D_MODEL), F32),
        scratch_shapes=[pltpu.VMEM((FFN_TILE // SUB_TILE, SUB_TILE, D_FF), BF16)],
        compiler_params=pltpu.CompilerParams(
            dimension_semantics=("arbitrary",), vmem_limit_bytes=VMEM_LIMIT_BYTES),
        name="ffn",
    )(x, p["ffn_gpre"], p["ffn_win"], p["ffn_wout"], p["ffn_gpost"])


def _layer_params(l, norm_mix_pre, norm_mix_post, norm_ffn_pre, norm_ffn_post, w_in, conv_w,
                  pool_w, pool_scale, sg_norm, sg_w, sg_b, attn_sink, w_branch, w_gate, b_gate,
                  w_out, w_ffn_in, w_ffn_out):
    poolw = jnp.zeros((256, 256), F32)
    for g in range(256 // POOL_GDIM):
        poolw = poolw.at[g * POOL_GDIM:(g + 1) * POOL_GDIM,
                         g * POOL_GDIM:(g + 1) * POOL_GDIM].set(pool_w[l, g])
    return dict(
        w_in=w_in[l].astype(BF16),
        gpre=norm_mix_pre[l].reshape(1, D_MODEL),
        gpost=norm_mix_post[l].reshape(1, D_MODEL),
        convw=conv_w[l],
        poolw=poolw.astype(BF16),
        poolscale=pool_scale[l].reshape(1, 256),
        sgnorm=sg_norm[l].reshape(1, 256),
        sgwt=jnp.swapaxes(sg_w[l], 1, 2).astype(BF16),
        sgb=jnp.repeat(sg_b[l].T, SG_GDIM, axis=1),
        sink=attn_sink[l],
        wg=w_gate[l].astype(BF16),
        bg=b_gate[l],
        wb=w_branch[l].astype(BF16),
        wo=w_out[l].reshape(N_COL_CHUNKS, MXU, D_MODEL).astype(BF16),
        ffn_gpre=norm_ffn_pre[l].reshape(1, D_MODEL),
        ffn_gpost=norm_ffn_post[l].reshape(1, D_MODEL),
        ffn_win=w_ffn_in[l].astype(BF16),
        ffn_wout=w_ffn_out[l].astype(BF16),
    )


def _trunk(x, layers):
    b, s, d = x.shape
    x = x.reshape(b * s, d)
    for p in layers:
        h, z = _in_proj(x, p["gpre"], p["w_in"])
        x = _mix(x, h, z, p, s)
        x = _ffn(x, p)
    return x.reshape(b, s, d)


def kernel(x_prompt, x_sample, norm_mix_pre, norm_mix_post, norm_ffn_pre, norm_ffn_post, w_in,
           conv_w, pool_w, pool_scale, sg_norm, sg_w, sg_b, attn_sink, w_branch, w_gate, b_gate,
           w_out, w_ffn_in, w_ffn_out):
    depth = w_in.shape[0]
    layers = [
        _layer_params(l, norm_mix_pre, norm_mix_post, norm_ffn_pre, norm_ffn_post, w_in, conv_w,
                      pool_w, pool_scale, sg_norm, sg_w, sg_b, attn_sink, w_branch, w_gate,
                      b_gate, w_out, w_ffn_in, w_ffn_out)
        for l in range(depth)]
    return (_trunk(x_prompt, layers), _trunk(x_sample, layers))
```

```python
import functools

import jax
import jax.numpy as jnp
from jax import lax
from jax.experimental import pallas as pl
from jax.experimental.pallas import tpu as pltpu

D_MODEL = 1024
EPS = 1e-6
BRANCH_WIDTH = 256
N_BRANCH = 4
POOL_GDIM = 64
CHUNK = 128
SG_GROUPS = 4
SG_GDIM = 64
N_Q_HEADS = 4
N_KV_HEADS = 2
HEAD_DIM = 64
WINDOW = 128
ATTN_BLOCK = 128
NEG_BIG = -1e30
D_IN = 2048
D_FF = 2816

COL_BGATE, COL_CGATE, COL_XIN, COL_POOL = 0, 256, 512, 768
COL_SG_U, COL_SG_V, COL_Q, COL_K, COL_V = 1024, 1280, 1536, 1792, 1920
KV_COL_BLOCK = COL_K // 256

LANES = 128
MXU = 256
HALO_ROWS = 16
N_COL_CHUNKS = D_MODEL // MXU
N_FF_CHUNKS = D_FF // MXU
TILE = 512
MIX_STEP = 1024
IN_TILE = 2048
FFN_TILE = 1024
SUB_TILE = 512
VMEM_LIMIT_BYTES = 56 * 1024 * 1024
MIX_VMEM_LIMIT_BYTES = 58 * 1024 * 1024

BF16 = jnp.bfloat16
F32 = jnp.float32


def _dot(a, b):
    return jnp.dot(a, b, preferred_element_type=F32)


def _rms_norm(x, gain):
    ms = jnp.mean(x * x, axis=-1, keepdims=True)
    return x * lax.rsqrt(ms + EPS) * gain


def _const_spec(shape):
    nd = len(shape)
    return pl.BlockSpec(shape, lambda i: (0,) * nd, pipeline_mode=pl.Buffered(1))


def _in_proj_kernel(x_ref, gain_ref, w_ref, h_ref, z_ref):
    for r in range(IN_TILE // SUB_TILE):
        rows = slice(r * SUB_TILE, (r + 1) * SUB_TILE)
        h = _rms_norm(x_ref[rows, :], gain_ref[...]).astype(BF16)
        h_ref[rows, :] = h
        z_ref[rows, :] = _dot(h, w_ref[...]).astype(BF16)


def _in_proj(x, gain, w_in):
    n = x.shape[0]
    return pl.pallas_call(
        _in_proj_kernel,
        grid=(n // IN_TILE,),
        in_specs=[
            pl.BlockSpec((IN_TILE, D_MODEL), lambda i: (i, 0)),
            _const_spec((1, D_MODEL)),
            _const_spec((D_MODEL, D_IN)),
        ],
        out_specs=[pl.BlockSpec((IN_TILE, D_MODEL), lambda i: (i, 0)),
                   pl.BlockSpec((IN_TILE, D_IN), lambda i: (i, 0))],
        out_shape=[jax.ShapeDtypeStruct((n, D_MODEL), BF16),
                   jax.ShapeDtypeStruct((n, D_IN), BF16)],
        compiler_params=pltpu.CompilerParams(
            dimension_semantics=("arbitrary",), vmem_limit_bytes=VMEM_LIMIT_BYTES),
        name="in_proj",
    )(x, gain, w_in)


def _attn_bias_t():
    key = lax.broadcasted_iota(jnp.int32, (3 * ATTN_BLOCK, N_Q_HEADS * ATTN_BLOCK), 0)
    col = lax.broadcasted_iota(jnp.int32, (3 * ATTN_BLOCK, N_Q_HEADS * ATTN_BLOCK), 1)
    head = sum((col >= hd * ATTN_BLOCK).astype(jnp.int32) for hd in range(1, N_Q_HEADS))
    rel = jnp.abs(col - head * ATTN_BLOCK - key + ATTN_BLOCK)
    slope = jnp.full(key.shape, 2.0 ** (-8.0 / N_Q_HEADS), F32)
    for hd in range(1, N_Q_HEADS):
        slope = jnp.where(head == hd, 2.0 ** (-8.0 * (hd + 1) / N_Q_HEADS), slope)
    return jnp.where(rel <= WINDOW, -slope * rel.astype(F32), NEG_BIG)


def _mix_kernel(x_ref, h_ref, z_ref, zprev_ref, znext_ref, kvprev_ref, kvnext_ref,
                convw_ref, poolw_ref, poolscale_ref, sgnorm_ref, sgwt_ref, sgb_ref,
                sink_ref, wg_ref, bg_ref, wb_ref, wo_ref, gpost_ref,
                o_ref,
                uext_ref, pext_ref, psum_ref, kvext_ref, vt_ref, qt_ref, br_ref, bias_ref,
                merged_ref,
                *, seq_len):
    step, tile = MIX_STEP, TILE
    i = pl.program_id(0)
    steps_per_seq = seq_len // step
    s_in_seq = lax.rem(i, steps_per_seq)
    is_first = s_in_seq == 0
    is_last = s_in_seq == steps_per_seq - 1
    n_blocks = step // ATTN_BLOCK

    @pl.when(i == 0)
    def _():
        bias_ref[...] = _attn_bias_t()

    def attn_prepare():
        kvext_ref[0:ATTN_BLOCK, :] = kvprev_ref[...]
        kvext_ref[ATTN_BLOCK:ATTN_BLOCK + step, :] = z_ref[:, COL_K:COL_K + 256]
        kvext_ref[ATTN_BLOCK + step:, :] = kvnext_ref[...]
        vt_ref[...] = kvext_ref[:, LANES:].astype(F32).T.astype(BF16)
        qt_ref[...] = z_ref[:, COL_Q:COL_Q + 256].astype(F32).T.astype(BF16)

    key_row = lax.broadcasted_iota(jnp.int32, (3 * ATTN_BLOCK, N_Q_HEADS * ATTN_BLOCK), 0)
    head_col = lax.broadcasted_iota(jnp.int32, (1, N_Q_HEADS * ATTN_BLOCK), 1)
    sink_row = jnp.full((1, N_Q_HEADS * ATTN_BLOCK), sink_ref[0], F32)
    for hd in range(1, N_Q_HEADS):
        sink_row = jnp.where(head_col >= hd * ATTN_BLOCK, sink_ref[hd], sink_row)

    def attn_scores(b):
        r0 = b * ATTN_BLOCK
        keys = kvext_ref[r0:r0 + 3 * ATTN_BLOCK, 0:LANES]
        qt = [qt_ref[hd * HEAD_DIM:(hd + 1) * HEAD_DIM, r0:r0 + ATTN_BLOCK]
              for hd in range(N_Q_HEADS)]
        zero = jnp.zeros((HEAD_DIM, ATTN_BLOCK), BF16)
        rhs = jnp.concatenate(
            [jnp.concatenate([qt[0], qt[1], zero, zero], axis=1),
             jnp.concatenate([zero, zero, qt[2], qt[3]], axis=1)], axis=0)
        s = _dot(keys, rhs) * (HEAD_DIM ** -0.5) + bias_ref[...]
        if b == 0:
            s = s + jnp.where(jnp.logical_and(is_first, key_row < ATTN_BLOCK), NEG_BIG, 0.0)
        if b == n_blocks - 1:
            s = s + jnp.where(jnp.logical_and(is_last, key_row >= 2 * ATTN_BLOCK), NEG_BIG, 0.0)
        return s

    def attn_values(b, s):
        r0 = b * ATTN_BLOCK
        m = jnp.maximum(jnp.max(s, axis=0, keepdims=True), sink_row)
        p = jnp.exp(s - m)
        denom = jnp.sum(p, axis=0, keepdims=True) + jnp.exp(sink_row - m)
        p = p.astype(BF16)
        for kh in range(N_KV_HEADS):
            cols = slice(kh * 2 * ATTN_BLOCK, (kh + 1) * 2 * ATTN_BLOCK)
            vt = vt_ref[kh * HEAD_DIM:(kh + 1) * HEAD_DIM, r0:r0 + 3 * ATTN_BLOCK]
            o_t = _dot(vt, p[:, cols]) / denom[:, cols]
            both = jnp.concatenate([o_t[:, :ATTN_BLOCK], o_t[:, ATTN_BLOCK:]], axis=0)
            br_ref[3, r0:r0 + ATTN_BLOCK, kh * LANES:(kh + 1) * LANES] = both.T.astype(BF16)

    def fill_with_halos(dst_ref, sub, value_of):
        split = tile + HALO_ROWS
        if sub == 0:
            dst_ref[0:HALO_ROWS, :] = jnp.where(is_first, 0.0, value_of(zprev_ref, slice(None)))
            dst_ref[HALO_ROWS:HALO_ROWS + split, :] = value_of(z_ref, slice(0, split))
        else:
            dst_ref[HALO_ROWS + split:HALO_ROWS + step, :] = value_of(z_ref, slice(split, step))
            dst_ref[HALO_ROWS + step:, :] = jnp.where(is_last, 0.0, value_of(znext_ref, slice(None)))

    def conv_fill(sub):
        def gated_u(ref, rows):
            c = ref[rows, COL_CGATE:COL_CGATE + 256].astype(F32)
            xin = ref[rows, COL_XIN:COL_XIN + 256].astype(F32)
            return c * xin

        fill_with_halos(uext_ref, sub, gated_u)

    def conv_rows(sub, half):
        n = tile // 2
        r0 = sub * tile + half * n
        conv = (convw_ref[0:1, :] * uext_ref[HALO_ROWS - 1 + r0:HALO_ROWS - 1 + r0 + n, :]
                + convw_ref[1:2, :] * uext_ref[HALO_ROWS + r0:HALO_ROWS + r0 + n, :]
                + convw_ref[2:3, :] * uext_ref[HALO_ROWS + 1 + r0:HALO_ROWS + 1 + r0 + n, :])
        bgate = z_ref[r0:r0 + n, COL_BGATE:COL_BGATE + 256].astype(F32)
        br_ref[0, r0:r0 + n, :] = (bgate * conv).astype(BF16)

    def pool_fill(sub):
        fill_with_halos(pext_ref, sub,
                        lambda ref, rows: ref[rows, COL_POOL:COL_POOL + 256].astype(F32))

    def pool_sums_wide(sub, half):
        lo, hi = (0, tile + HALO_ROWS) if sub == 0 else (tile + HALO_ROWS, step + HALO_ROWS)
        rows = (hi - lo) // 2
        r0 = lo + half * rows
        acc = pext_ref[HALO_ROWS - 8 + r0:HALO_ROWS - 8 + r0 + rows, LANES:]
        for back in range(1, 8):
            acc = acc + pext_ref[HALO_ROWS - back + r0:HALO_ROWS - back + r0 + rows, LANES:]
        psum_ref[r0:r0 + rows, :] = acc

    def inv_count(w, row0, rows):
        pos = s_in_seq * step + row0 + lax.broadcasted_iota(jnp.int32, (rows, LANES), 0)
        hi = jnp.minimum(pos + w // 2, seq_len)
        lo = jnp.maximum(pos - w // 2, 0)
        return 1.0 / (hi - lo).astype(F32)

    def pool_branch(sub, half):
        def shifted(off, rows):
            return pext_ref[HALO_ROWS + off + rows.start:HALO_ROWS + off + rows.stop, 0:LANES]

        def pooled_rows(rows, exact_counts):
            n = rows.stop - rows.start
            lane = lax.broadcasted_iota(jnp.int32, (n, LANES), 1)
            s2 = shifted(-1, rows) + shifted(0, rows)
            s4 = s2 + shifted(-2, rows) + shifted(1, rows)
            s8 = psum_ref[rows.start + 4:rows.stop + 4, :]
            s16 = psum_ref[rows.start:rows.stop, :] + psum_ref[rows.start + 8:rows.stop + 8, :]
            if exact_counts:
                inv = [inv_count(w, rows.start, n) for w in (2, 4, 8, 16)]
            else:
                inv = [1.0 / w for w in (2, 4, 8, 16)]
            mean = jnp.concatenate(
                [jnp.where(lane < POOL_GDIM, s2 * inv[0], s4 * inv[1]),
                 jnp.where(lane < POOL_GDIM, s8 * inv[2], s16 * inv[3])], axis=1)
            return mean - pext_ref[HALO_ROWS + rows.start:HALO_ROWS + rows.stop, :]

        r0 = sub * tile + half * (tile // 2)
        r1 = r0 + tile // 2
        if r0 == 0:
            pooled = jnp.concatenate(
                [pooled_rows(slice(0, HALO_ROWS), True),
                 pooled_rows(slice(HALO_ROWS, r1), False)], axis=0)
        elif r1 == step:
            pooled = jnp.concatenate(
                [pooled_rows(slice(r0, step - HALO_ROWS), False),
                 pooled_rows(slice(step - HALO_ROWS, step), True)], axis=0)
        else:
            pooled = pooled_rows(slice(r0, r1), False)
        br_ref[1, r0:r1, :] = (
            _dot(pooled.astype(BF16), poolw_ref[...]) * poolscale_ref[...]).astype(BF16)

    def gating_norm(sub):
        v = z_ref[sub * tile:(sub + 1) * tile, COL_SG_V:COL_SG_V + 256].astype(F32)
        mu = jnp.mean(v, axis=-1, keepdims=True)
        vc = v - mu
        var = jnp.mean(vc * vc, axis=-1, keepdims=True)
        return (vc * lax.rsqrt(var + EPS) * sgnorm_ref[...]).T

    def gating_mix(sub, vln_t):
        n_chunks = tile // CHUNK
        mixed_t = []
        for g in range(SG_GROUPS):
            lhs = jnp.concatenate(
                [vln_t[g * SG_GDIM:(g + 1) * SG_GDIM, n * CHUNK:(n + 1) * CHUNK]
                 for n in range(n_chunks)], axis=0)
            mixed_t.append(_dot(lhs.astype(BF16), sgwt_ref[g]))
        for n in range(n_chunks):
            rows = slice(sub * tile + n * CHUNK, sub * tile + (n + 1) * CHUNK)
            tiles = []
            for j in range(2):
                piece = jnp.concatenate(
                    [mixed_t[2 * j][n * SG_GDIM:(n + 1) * SG_GDIM],
                     mixed_t[2 * j + 1][n * SG_GDIM:(n + 1) * SG_GDIM]], axis=0)
                tiles.append(piece.T)
            mixed = jnp.concatenate(tiles, axis=1) + sgb_ref[...]
            u = z_ref[rows, COL_SG_U:COL_SG_U + 256].astype(F32)
            br_ref[2, rows, :] = (u * mixed).astype(BF16)

    def gate_of(b, c, h):
        cols = slice(c * MXU, (c + 1) * MXU)
        return jax.nn.sigmoid(_dot(h, wg_ref[b, :, cols]) + bg_ref[b:b + 1, cols])

    def gated_project(sub, b, c, h, gate=None):
        cols = slice(c * MXU, (c + 1) * MXU)
        if gate is None:
            gate = gate_of(b, c, h)
        term = gate * _dot(br_ref[b, sub * tile:(sub + 1) * tile, :], wb_ref[b, :, cols])
        if b == 0:
            merged_ref[:, cols] = term
        else:
            merged_ref[:, cols] += term

    def finish(sub, half):
        n = tile // 2
        y = jnp.zeros((n, D_MODEL), F32)
        for c in range(N_COL_CHUNKS):
            y = y + _dot(merged_ref[half * n:(half + 1) * n, c * MXU:(c + 1) * MXU].astype(BF16),
                         wo_ref[c])
        rows = slice(sub * tile + half * n, sub * tile + (half + 1) * n)
        o_ref[rows, :] = x_ref[rows, :] + _rms_norm(y, gpost_ref[...])

    def mix_sub_tile(sub):
        blk = sub * (tile // ATTN_BLOCK)
        sc0 = attn_scores(blk)
        sc1 = attn_scores(blk + 1)
        h = h_ref[sub * tile:(sub + 1) * tile, :]
        gate0 = gate_of(0, 0, h)
        conv_fill(sub)
        gate1 = gate_of(0, 1, h)
        conv_rows(sub, 0)
        conv_rows(sub, 1)

        gated_project(sub, 0, 0, h, gate0)
        vln_t = gating_norm(sub)
        gated_project(sub, 0, 1, h, gate1)
        gating_mix(sub, vln_t)
        gated_project(sub, 0, 2, h)
        attn_values(blk, sc0)
        gated_project(sub, 0, 3, h)

        gated_project(sub, 2, 0, h)
        pool_fill(sub)
        pool_sums_wide(sub, 0)
        pool_sums_wide(sub, 1)
        gated_project(sub, 2, 1, h)
        pool_branch(sub, 0)
        gated_project(sub, 2, 2, h)
        pool_branch(sub, 1)
        attn_values(blk + 1, sc1)
        gated_project(sub, 2, 3, h)

        gated_project(sub, 1, 0, h)
        sc2 = attn_scores(blk + 2)
        gated_project(sub, 1, 1, h)
        attn_values(blk + 2, sc2)
        sc3 = attn_scores(blk + 3)
        gated_project(sub, 1, 2, h)
        attn_values(blk + 3, sc3)
        gated_project(sub, 1, 3, h)

        for c in range(N_COL_CHUNKS):
            gated_project(sub, 3, c, h)

        finish(sub, 0)
        finish(sub, 1)

    attn_prepare()
    for sub in range(step // tile):
        mix_sub_tile(sub)


def _mix(x, h, z, p, seq_len):
    n = x.shape[0]
    step = MIX_STEP
    assert TILE == N_BRANCH * ATTN_BLOCK and step == 2 * TILE and seq_len % step == 0
    r16 = step // HALO_ROWS
    r128 = step // ATTN_BLOCK
    last16 = n // HALO_ROWS - 1
    last128 = n // ATTN_BLOCK - 1
    kernel = functools.partial(_mix_kernel, seq_len=seq_len)
    return pl.pallas_call(
        kernel,
        grid=(n // step,),
        in_specs=[
            pl.BlockSpec((step, D_MODEL), lambda i: (i, 0)),
            pl.BlockSpec((step, D_MODEL), lambda i: (i, 0)),
            pl.BlockSpec((step, D_IN), lambda i: (i, 0)),
            pl.BlockSpec((HALO_ROWS, 1024), lambda i: (jnp.maximum(i * r16 - 1, 0), 0)),
            pl.BlockSpec((HALO_ROWS, 1024), lambda i: (jnp.minimum((i + 1) * r16, last16), 0)),
            pl.BlockSpec((ATTN_BLOCK, 256),
                         lambda i: (jnp.maximum(i * r128 - 1, 0), KV_COL_BLOCK)),
            pl.BlockSpec((ATTN_BLOCK, 256),
                         lambda i: (jnp.minimum((i + 1) * r128, last128), KV_COL_BLOCK)),
            _const_spec((3, 256)),
            _const_spec((256, 256)),
            _const_spec((1, 256)),
            _const_spec((1, 256)),
            _const_spec((SG_GROUPS, CHUNK, CHUNK)),
            _const_spec((CHUNK, 256)),
            pl.BlockSpec(memory_space=pltpu.SMEM),
            _const_spec((N_BRANCH, D_MODEL, D_MODEL)),
            _const_spec((N_BRANCH, D_MODEL)),
            _const_spec((N_BRANCH, BRANCH_WIDTH, D_MODEL)),
            _const_spec((N_COL_CHUNKS, MXU, D_MODEL)),
            _const_spec((1, D_MODEL)),
        ],
        out_specs=pl.BlockSpec((step, D_MODEL), lambda i: (i, 0)),
        out_shape=jax.ShapeDtypeStruct((n, D_MODEL), F32),
        scratch_shapes=[
            pltpu.VMEM((step + 2 * HALO_ROWS, 256), F32),
            pltpu.VMEM((step + 2 * HALO_ROWS, 256), F32),
            pltpu.VMEM((step + HALO_ROWS, LANES), F32),
            pltpu.VMEM((step + 2 * ATTN_BLOCK, 256), BF16),
            pltpu.VMEM((LANES, step + 2 * ATTN_BLOCK), BF16),
            pltpu.VMEM((N_Q_HEADS * HEAD_DIM, step), BF16),
            pltpu.VMEM((N_BRANCH, step, BRANCH_WIDTH), BF16),
            pltpu.VMEM((3 * ATTN_BLOCK, N_Q_HEADS * ATTN_BLOCK), F32),
            pltpu.VMEM((TILE, D_MODEL), F32),
        ],
        compiler_params=pltpu.CompilerParams(
            dimension_semantics=("arbitrary",), vmem_limit_bytes=MIX_VMEM_LIMIT_BYTES),
        name="mix",
    )(x, h, z, z, z, z, z,
      p["convw"], p["poolw"], p["poolscale"], p["sgnorm"], p["sgwt"], p["sgb"],
      p["sink"], p["wg"], p["bg"], p["wb"], p["wo"], p["gpost"])


def _ffn_kernel(x_ref, gpre_ref, win_ref, wout_ref, gpost_ref, o_ref, act_ref):
    for r in range(FFN_TILE // SUB_TILE):
        rows = slice(r * SUB_TILE, (r + 1) * SUB_TILE)
        x = x_ref[rows, :]
        h = _rms_norm(x, gpre_ref[...]).astype(BF16)
        for c in range(N_FF_CHUNKS):
            a = _dot(h, win_ref[:, c * MXU:(c + 1) * MXU])
            g = _dot(h, win_ref[:, D_FF + c * MXU:D_FF + (c + 1) * MXU])
            act_ref[r, :, c * MXU:(c + 1) * MXU] = (a * jax.nn.sigmoid(a) * g).astype(BF16)
        y = _dot(act_ref[r], wout_ref[...])
        o_ref[rows, :] = x + _rms_norm(y, gpost_ref[...])


def _ffn(x, p):
    n = x.shape[0]
    return pl.pallas_call(
        _ffn_kernel,
        grid=(n // FFN_TILE,),
        in_specs=[
            pl.BlockSpec((FFN_TILE, D_MODEL), lambda i: (i, 0)),
            _const_spec((1, D_MODEL)),
            _const_spec((D_MODEL, 2 * D_FF)),
            _const_spec((D_FF, D_MODEL)),
            _const_spec((1, D_MODEL)),
        ],
        out_specs=pl.BlockSpec((FFN_TILE, D_MODEL), lambda i: (i, 0)),
        out_shape=jax.ShapeDtypeStruct((n, D_MODEL), F32),
        scratch_shapes=[pltpu.VMEM((FFN_TILE // SUB_TILE, SUB_TILE, D_FF), BF16)],
        compiler_params=pltpu.CompilerParams(
            dimension_semantics=("arbitrary",), vmem_limit_bytes=VMEM_LIMIT_BYTES),
        name="ffn",
    )(x, p["ffn_gpre"], p["ffn_win"], p["ffn_wout"], p["ffn_gpost"])


def _layer_params(l, norm_mix_pre, norm_mix_post, norm_ffn_pre, norm_ffn_post, w_in, conv_w,
                  pool_w, pool_scale, sg_norm, sg_w, sg_b, attn_sink, w_branch, w_gate, b_gate,
                  w_out, w_ffn_in, w_ffn_out):
    poolw = jnp.zeros((256, 256), F32)
    for g in range(256 // POOL_GDIM):
        poolw = poolw.at[g * POOL_GDIM:(g + 1) * POOL_GDIM,
                         g * POOL_GDIM:(g + 1) * POOL_GDIM].set(pool_w[l, g])
    return dict(
        w_in=w_in[l].astype(BF16),
        gpre=norm_mix_pre[l].reshape(1, D_MODEL),
        gpost=norm_mix_post[l].reshape(1, D_MODEL),
        convw=conv_w[l],
        poolw=poolw.astype(BF16),
        poolscale=pool_scale[l].reshape(1, 256),
        sgnorm=sg_norm[l].reshape(1, 256),
        sgwt=jnp.swapaxes(sg_w[l], 1, 2).astype(BF16),
        sgb=jnp.repeat(sg_b[l].T, SG_GDIM, axis=1),
        sink=attn_sink[l],
        wg=w_gate[l].astype(BF16),
        bg=b_gate[l],
        wb=w_branch[l].astype(BF16),
        wo=w_out[l].reshape(N_COL_CHUNKS, MXU, D_MODEL).astype(BF16),
        ffn_gpre=norm_ffn_pre[l].reshape(1, D_MODEL),
        ffn_gpost=norm_ffn_post[l].reshape(1, D_MODEL),
        ffn_win=w_ffn_in[l].astype(BF16),
        ffn_wout=w_ffn_out[l].astype(BF16),
    )


def _trunk(x, layers):
    b, s, d = x.shape
    x = x.reshape(b * s, d)
    for p in layers:
        h, z = _in_proj(x, p["gpre"], p["w_in"])
        x = _mix(x, h, z, p, s)
        x = _ffn(x, p)
    return x.reshape(b, s, d)


def kernel(x_prompt, x_sample, norm_mix_pre, norm_mix_post, norm_ffn_pre, norm_ffn_post, w_in,
           conv_w, pool_w, pool_scale, sg_norm, sg_w, sg_b, attn_sink, w_branch, w_gate, b_gate,
           w_out, w_ffn_in, w_ffn_out):
    depth = w_in.shape[0]
    layers = [
        _layer_params(l, norm_mix_pre, norm_mix_post, norm_ffn_pre, norm_ffn_post, w_in, conv_w,
                      pool_w, pool_scale, sg_norm, sg_w, sg_b, attn_sink, w_branch, w_gate,
                      b_gate, w_out, w_ffn_in, w_ffn_out)
        for l in range(depth)]
    return (_trunk(x_prompt, layers), _trunk(x_sample, layers))
```

```python
import functools

import jax
import jax.numpy as jnp
from jax import lax
from jax.experimental import pallas as pl
from jax.experimental.pallas import tpu as pltpu

D_MODEL = 1024
EPS = 1e-6
BRANCH_WIDTH = 256
N_BRANCH = 4
POOL_GDIM = 64
CHUNK = 128
SG_GROUPS = 4
SG_GDIM = 64
N_Q_HEADS = 4
N_KV_HEADS = 2
HEAD_DIM = 64
WINDOW = 128
ATTN_BLOCK = 128
NEG_BIG = -1e30
D_IN = 2048
D_FF = 2816

COL_BGATE, COL_CGATE, COL_XIN, COL_POOL = 0, 256, 512, 768
COL_SG_U, COL_SG_V, COL_Q, COL_K, COL_V = 1024, 1280, 1536, 1792, 1920
KV_COL_BLOCK = COL_K // 256

LANES = 128
MXU = 256
HALO_ROWS = 16
N_COL_CHUNKS = D_MODEL // MXU
N_FF_CHUNKS = D_FF // MXU
TILE = 512
MIX_STEP = 1024
IN_TILE = 2048
FFN_TILE = 1024
SUB_TILE = 512
VMEM_LIMIT_BYTES = 56 * 1024 * 1024
MIX_VMEM_LIMIT_BYTES = 58 * 1024 * 1024

BF16 = jnp.bfloat16
F32 = jnp.float32


def _dot(a, b):
    return jnp.dot(a, b, preferred_element_type=F32)


def _rms_norm(x, gain):
    ms = jnp.mean(x * x, axis=-1, keepdims=True)
    return x * lax.rsqrt(ms + EPS) * gain


def _const_spec(shape):
    nd = len(shape)
    return pl.BlockSpec(shape, lambda i: (0,) * nd, pipeline_mode=pl.Buffered(1))


def _in_proj_kernel(x_ref, gain_ref, w_ref, h_ref, z_ref):
    for r in range(IN_TILE // SUB_TILE):
        rows = slice(r * SUB_TILE, (r + 1) * SUB_TILE)
        h = _rms_norm(x_ref[rows, :], gain_ref[...]).astype(BF16)
        h_ref[rows, :] = h
        z_ref[rows, :] = _dot(h, w_ref[...]).astype(BF16)


def _in_proj(x, gain, w_in):
    n = x.shape[0]
    return pl.pallas_call(
        _in_proj_kernel,
        grid=(n // IN_TILE,),
        in_specs=[
            pl.BlockSpec((IN_TILE, D_MODEL), lambda i: (i, 0)),
            _const_spec((1, D_MODEL)),
            _const_spec((D_MODEL, D_IN)),
        ],
        out_specs=[pl.BlockSpec((IN_TILE, D_MODEL), lambda i: (i, 0)),
                   pl.BlockSpec((IN_TILE, D_IN), lambda i: (i, 0))],
        out_shape=[jax.ShapeDtypeStruct((n, D_MODEL), BF16),
                   jax.ShapeDtypeStruct((n, D_IN), BF16)],
        compiler_params=pltpu.CompilerParams(
            dimension_semantics=("arbitrary",), vmem_limit_bytes=VMEM_LIMIT_BYTES),
        name="in_proj",
    )(x, gain, w_in)


def _attn_bias_t():
    key = lax.broadcasted_iota(jnp.int32, (3 * ATTN_BLOCK, N_Q_HEADS * ATTN_BLOCK), 0)
    col = lax.broadcasted_iota(jnp.int32, (3 * ATTN_BLOCK, N_Q_HEADS * ATTN_BLOCK), 1)
    head = sum((col >= hd * ATTN_BLOCK).astype(jnp.int32) for hd in range(1, N_Q_HEADS))
    rel = jnp.abs(col - head * ATTN_BLOCK - key + ATTN_BLOCK)
    slope = jnp.full(key.shape, 2.0 ** (-8.0 / N_Q_HEADS), F32)
    for hd in range(1, N_Q_HEADS):
        slope = jnp.where(head == hd, 2.0 ** (-8.0 * (hd + 1) / N_Q_HEADS), slope)
    return jnp.where(rel <= WINDOW, -slope * rel.astype(F32), NEG_BIG)


def _mix_kernel(x_ref, h_ref, z_ref, zprev_ref, znext_ref, kvprev_ref, kvnext_ref,
                convw_ref, poolw_ref, poolscale_ref, sgnorm_ref, sgwt_ref, sgb_ref,
                sink_ref, wg_ref, bg_ref, wb_ref, wo_ref, gpost_ref,
                o_ref,
                uext_ref, pext_ref, psum_ref, kvext_ref, vt_ref, qt_ref, br_ref, bias_ref,
                merged_ref,
                *, seq_len):
    step, tile = MIX_STEP, TILE
    i = pl.program_id(0)
    steps_per_seq = seq_len // step
    s_in_seq = lax.rem(i, steps_per_seq)
    is_first = s_in_seq == 0
    is_last = s_in_seq == steps_per_seq - 1
    n_blocks = step // ATTN_BLOCK

    @pl.when(i == 0)
    def _():
        bias_ref[...] = _attn_bias_t()

    def attn_prepare():
        kvext_ref[0:ATTN_BLOCK, :] = kvprev_ref[...]
        kvext_ref[ATTN_BLOCK:ATTN_BLOCK + step, :] = z_ref[:, COL_K:COL_K + 256]
        kvext_ref[ATTN_BLOCK + step:, :] = kvnext_ref[...]
        vt_ref[...] = kvext_ref[:, LANES:].astype(F32).T.astype(BF16)
        qt_ref[...] = z_ref[:, COL_Q:COL_Q + 256].astype(F32).T.astype(BF16)

    key_row = lax.broadcasted_iota(jnp.int32, (3 * ATTN_BLOCK, N_Q_HEADS * ATTN_BLOCK), 0)
    head_col = lax.broadcasted_iota(jnp.int32, (1, N_Q_HEADS * ATTN_BLOCK), 1)
    sink_row = jnp.full((1, N_Q_HEADS * ATTN_BLOCK), sink_ref[0], F32)
    for hd in range(1, N_Q_HEADS):
        sink_row = jnp.where(head_col >= hd * ATTN_BLOCK, sink_ref[hd], sink_row)

    def attn_scores(b):
        r0 = b * ATTN_BLOCK
        keys = kvext_ref[r0:r0 + 3 * ATTN_BLOCK, 0:LANES]
        qt = [qt_ref[hd * HEAD_DIM:(hd + 1) * HEAD_DIM, r0:r0 + ATTN_BLOCK]
              for hd in range(N_Q_HEADS)]
        zero = jnp.zeros((HEAD_DIM, ATTN_BLOCK), BF16)
        rhs = jnp.concatenate(
            [jnp.concatenate([qt[0], qt[1], zero, zero], axis=1),
             jnp.concatenate([zero, zero, qt[2], qt[3]], axis=1)], axis=0)
        s = _dot(keys, rhs) * (HEAD_DIM ** -0.5) + bias_ref[...]
        if b == 0:
            s = s + jnp.where(jnp.logical_and(is_first, key_row < ATTN_BLOCK), NEG_BIG, 0.0)
        if b == n_blocks - 1:
            s = s + jnp.where(jnp.logical_and(is_last, key_row >= 2 * ATTN_BLOCK), NEG_BIG, 0.0)
        return s

    def attn_values(b, s):
        r0 = b * ATTN_BLOCK
        m = jnp.maximum(jnp.max(s, axis=0, keepdims=True), sink_row)
        p = jnp.exp(s - m)
        denom = jnp.sum(p, axis=0, keepdims=True) + jnp.exp(sink_row - m)
        p = p.astype(BF16)
        for kh in range(N_KV_HEADS):
            cols = slice(kh * 2 * ATTN_BLOCK, (kh + 1) * 2 * ATTN_BLOCK)
            vt = vt_ref[kh * HEAD_DIM:(kh + 1) * HEAD_DIM, r0:r0 + 3 * ATTN_BLOCK]
            o_t = _dot(vt, p[:, cols]) / denom[:, cols]
            both = jnp.concatenate([o_t[:, :ATTN_BLOCK], o_t[:, ATTN_BLOCK:]], axis=0)
            br_ref[3, r0:r0 + ATTN_BLOCK, kh * LANES:(kh + 1) * LANES] = both.T.astype(BF16)

    def fill_with_halos(dst_ref, sub, value_of):
        split = tile + HALO_ROWS
        if sub == 0:
            dst_ref[0:HALO_ROWS, :] = jnp.where(is_first, 0.0, value_of(zprev_ref, slice(None)))
            dst_ref[HALO_ROWS:HALO_ROWS + split, :] = value_of(z_ref, slice(0, split))
        else:
            dst_ref[HALO_ROWS + split:HALO_ROWS + step, :] = value_of(z_ref, slice(split, step))
            dst_ref[HALO_ROWS + step:, :] = jnp.where(is_last, 0.0, value_of(znext_ref, slice(None)))

    def conv_fill(sub):
        def gated_u(ref, rows):
            c = ref[rows, COL_CGATE:COL_CGATE + 256].astype(F32)
            xin = ref[rows, COL_XIN:COL_XIN + 256].astype(F32)
            return c * xin

        fill_with_halos(uext_ref, sub, gated_u)

    def conv_rows(sub, half):
        n = tile // 2
        r0 = sub * tile + half * n
        conv = (convw_ref[0:1, :] * uext_ref[HALO_ROWS - 1 + r0:HALO_ROWS - 1 + r0 + n, :]
                + convw_ref[1:2, :] * uext_ref[HALO_ROWS + r0:HALO_ROWS + r0 + n, :]
                + convw_ref[2:3, :] * uext_ref[HALO_ROWS + 1 + r0:HALO_ROWS + 1 + r0 + n, :])
        bgate = z_ref[r0:r0 + n, COL_BGATE:COL_BGATE + 256].astype(F32)
        br_ref[0, r0:r0 + n, :] = (bgate * conv).astype(BF16)

    def pool_fill(sub):
        fill_with_halos(pext_ref, sub,
                        lambda ref, rows: ref[rows, COL_POOL:COL_POOL + 256].astype(F32))

    def pool_sums_wide(sub, half):
        lo, hi = (0, tile + HALO_ROWS) if sub == 0 else (tile + HALO_ROWS, step + HALO_ROWS)
        rows = (hi - lo) // 2
        r0 = lo + half * rows
        acc = pext_ref[HALO_ROWS - 8 + r0:HALO_ROWS - 8 + r0 + rows, LANES:]
        for back in range(1, 8):
            acc = acc + pext_ref[HALO_ROWS - back + r0:HALO_ROWS - back + r0 + rows, LANES:]
        psum_ref[r0:r0 + rows, :] = acc

    def inv_count(w, row0, rows):
        pos = s_in_seq * step + row0 + lax.broadcasted_iota(jnp.int32, (rows, LANES), 0)
        hi = jnp.minimum(pos + w // 2, seq_len)
        lo = jnp.maximum(pos - w // 2, 0)
        return 1.0 / (hi - lo).astype(F32)

    def pool_branch(sub, half):
        def shifted(off, rows):
            return pext_ref[HALO_ROWS + off + rows.start:HALO_ROWS + off + rows.stop, 0:LANES]

        def pooled_rows(rows, exact_counts):
            n = rows.stop - rows.start
            lane = lax.broadcasted_iota(jnp.int32, (n, LANES), 1)
            s2 = shifted(-1, rows) + shifted(0, rows)
            s4 = s2 + shifted(-2, rows) + shifted(1, rows)
            s8 = psum_ref[rows.start + 4:rows.stop + 4, :]
            s16 = psum_ref[rows.start:rows.stop, :] + psum_ref[rows.start + 8:rows.stop + 8, :]
            if exact_counts:
                inv = [inv_count(w, rows.start, n) for w in (2, 4, 8, 16)]
            else:
                inv = [1.0 / w for w in (2, 4, 8, 16)]
            mean = jnp.concatenate(
                [jnp.where(lane < POOL_GDIM, s2 * inv[0], s4 * inv[1]),
                 jnp.where(lane < POOL_GDIM, s8 * inv[2], s16 * inv[3])], axis=1)
            return mean - pext_ref[HALO_ROWS + rows.start:HALO_ROWS + rows.stop, :]

        r0 = sub * tile + half * (tile // 2)
        r1 = r0 + tile // 2
        if r0 == 0:
            pooled = jnp.concatenate(
                [pooled_rows(slice(0, HALO_ROWS), True),
                 pooled_rows(slice(HALO_ROWS, r1), False)], axis=0)
        elif r1 == step:
            pooled = jnp.concatenate(
                [pooled_rows(slice(r0, step - HALO_ROWS), False),
                 pooled_rows(slice(step - HALO_ROWS, step), True)], axis=0)
        else:
            pooled = pooled_rows(slice(r0, r1), False)
        br_ref[1, r0:r1, :] = (
            _dot(pooled.astype(BF16), poolw_ref[...]) * poolscale_ref[...]).astype(BF16)

    def gating_norm(sub):
        v = z_ref[sub * tile:(sub + 1) * tile, COL_SG_V:COL_SG_V + 256].astype(F32)
        mu = jnp.mean(v, axis=-1, keepdims=True)
        vc = v - mu
        var = jnp.mean(vc * vc, axis=-1, keepdims=True)
        return (vc * lax.rsqrt(var + EPS) * sgnorm_ref[...]).T

    def gating_mix(sub, vln_t):
        n_chunks = tile // CHUNK
        mixed_t = []
        for g in range(SG_GROUPS):
            lhs = jnp.concatenate(
                [vln_t[g * SG_GDIM:(g + 1) * SG_GDIM, n * CHUNK:(n + 1) * CHUNK]
                 for n in range(n_chunks)], axis=0)
            mixed_t.append(_dot(lhs.astype(BF16), sgwt_ref[g]))
        for n in range(n_chunks):
            rows = slice(sub * tile + n * CHUNK, sub * tile + (n + 1) * CHUNK)
            tiles = []
            for j in range(2):
                piece = jnp.concatenate(
                    [mixed_t[2 * j][n * SG_GDIM:(n + 1) * SG_GDIM],
                     mixed_t[2 * j + 1][n * SG_GDIM:(n + 1) * SG_GDIM]], axis=0)
                tiles.append(piece.T)
            mixed = jnp.concatenate(tiles, axis=1) + sgb_ref[...]
            u = z_ref[rows, COL_SG_U:COL_SG_U + 256].astype(F32)
            br_ref[2, rows, :] = (u * mixed).astype(BF16)

    def gated_project(sub, b, c, h):
        cols = slice(c * MXU, (c + 1) * MXU)
        gate = jax.nn.sigmoid(_dot(h, wg_ref[b, :, cols]) + bg_ref[b:b + 1, cols])
        term = gate * _dot(br_ref[b, sub * tile:(sub + 1) * tile, :], wb_ref[b, :, cols])
        if b == 0:
            merged_ref[:, cols] = term
        else:
            merged_ref[:, cols] += term

    def finish(sub, half):
        n = tile // 2
        y = jnp.zeros((n, D_MODEL), F32)
        for c in range(N_COL_CHUNKS):
            y = y + _dot(merged_ref[half * n:(half + 1) * n, c * MXU:(c + 1) * MXU].astype(BF16),
                         wo_ref[c])
        rows = slice(sub * tile + half * n, sub * tile + (half + 1) * n)
        o_ref[rows, :] = x_ref[rows, :] + _rms_norm(y, gpost_ref[...])

    def mix_sub_tile(sub):
        blk = sub * (tile // ATTN_BLOCK)
        sc0 = attn_scores(blk)
        sc1 = attn_scores(blk + 1)
        h = h_ref[sub * tile:(sub + 1) * tile, :]
        conv_fill(sub)
        conv_rows(sub, 0)
        conv_rows(sub, 1)

        gated_project(sub, 0, 0, h)
        vln_t = gating_norm(sub)
        gated_project(sub, 0, 1, h)
        gating_mix(sub, vln_t)
        gated_project(sub, 0, 2, h)
        attn_values(blk, sc0)
        gated_project(sub, 0, 3, h)

        gated_project(sub, 2, 0, h)
        pool_fill(sub)
        pool_sums_wide(sub, 0)
        pool_sums_wide(sub, 1)
        gated_project(sub, 2, 1, h)
        pool_branch(sub, 0)
        gated_project(sub, 2, 2, h)
        pool_branch(sub, 1)
        attn_values(blk + 1, sc1)
        gated_project(sub, 2, 3, h)

        gated_project(sub, 1, 0, h)
        sc2 = attn_scores(blk + 2)
        gated_project(sub, 1, 1, h)
        attn_values(blk + 2, sc2)
        sc3 = attn_scores(blk + 3)
        gated_project(sub, 1, 2, h)
        attn_values(blk + 3, sc3)
        gated_project(sub, 1, 3, h)

        for c in range(N_COL_CHUNKS):
            gated_project(sub, 3, c, h)

        finish(sub, 0)
        finish(sub, 1)

    attn_prepare()
    for sub in range(step // tile):
        mix_sub_tile(sub)


def _mix(x, h, z, p, seq_len):
    n = x.shape[0]
    step = MIX_STEP
    assert TILE == N_BRANCH * ATTN_BLOCK and step == 2 * TILE and seq_len % step == 0
    r16 = step // HALO_ROWS
    r128 = step // ATTN_BLOCK
    last16 = n // HALO_ROWS - 1
    last128 = n // ATTN_BLOCK - 1
    kernel = functools.partial(_mix_kernel, seq_len=seq_len)
    return pl.pallas_call(
        kernel,
        grid=(n // step,),
        in_specs=[
            pl.BlockSpec((step, D_MODEL), lambda i: (i, 0)),
            pl.BlockSpec((step, D_MODEL), lambda i: (i, 0)),
            pl.BlockSpec((step, D_IN), lambda i: (i, 0)),
            pl.BlockSpec((HALO_ROWS, 1024), lambda i: (jnp.maximum(i * r16 - 1, 0), 0)),
            pl.BlockSpec((HALO_ROWS, 1024), lambda i: (jnp.minimum((i + 1) * r16, last16), 0)),
            pl.BlockSpec((ATTN_BLOCK, 256),
                         lambda i: (jnp.maximum(i * r128 - 1, 0), KV_COL_BLOCK)),
            pl.BlockSpec((ATTN_BLOCK, 256),
                         lambda i: (jnp.minimum((i + 1) * r128, last128), KV_COL_BLOCK)),
            _const_spec((3, 256)),
            _const_spec((256, 256)),
            _const_spec((1, 256)),
            _const_spec((1, 256)),
            _const_spec((SG_GROUPS, CHUNK, CHUNK)),
            _const_spec((CHUNK, 256)),
            pl.BlockSpec(memory_space=pltpu.SMEM),
            _const_spec((N_BRANCH, D_MODEL, D_MODEL)),
            _const_spec((N_BRANCH, D_MODEL)),
            _const_spec((N_BRANCH, BRANCH_WIDTH, D_MODEL)),
            _const_spec((N_COL_CHUNKS, MXU, D_MODEL)),
            _const_spec((1, D_MODEL)),
        ],
        out_specs=pl.BlockSpec((step, D_MODEL), lambda i: (i, 0)),
        out_shape=jax.ShapeDtypeStruct((n, D_MODEL), F32),
        scratch_shapes=[
            pltpu.VMEM((step + 2 * HALO_ROWS, 256), F32),
            pltpu.VMEM((step + 2 * HALO_ROWS, 256), F32),
            pltpu.VMEM((step + HALO_ROWS, LANES), F32),
            pltpu.VMEM((step + 2 * ATTN_BLOCK, 256), BF16),
            pltpu.VMEM((LANES, step + 2 * ATTN_BLOCK), BF16),
            pltpu.VMEM((N_Q_HEADS * HEAD_DIM, step), BF16),
            pltpu.VMEM((N_BRANCH, step, BRANCH_WIDTH), BF16),
            pltpu.VMEM((3 * ATTN_BLOCK, N_Q_HEADS * ATTN_BLOCK), F32),
            pltpu.VMEM((TILE, D_MODEL), F32),
        ],
        compiler_params=pltpu.CompilerParams(
            dimension_semantics=("arbitrary",), vmem_limit_bytes=MIX_VMEM_LIMIT_BYTES),
        name="mix",
    )(x, h, z, z, z, z, z,
      p["convw"], p["poolw"], p["poolscale"], p["sgnorm"], p["sgwt"], p["sgb"],
      p["sink"], p["wg"], p["bg"], p["wb"], p["wo"], p["gpost"])


def _ffn_kernel(x_ref, gpre_ref, win_ref, wout_ref, gpost_ref, o_ref, act_ref):
    for r in range(FFN_TILE // SUB_TILE):
        rows = slice(r * SUB_TILE, (r + 1) * SUB_TILE)
        x = x_ref[rows, :]
        h = _rms_norm(x, gpre_ref[...]).astype(BF16)
        for c in range(N_FF_CHUNKS):
            a = _dot(h, win_ref[:, c * MXU:(c + 1) * MXU])
            g = _dot(h, win_ref[:, D_FF + c * MXU:D_FF + (c + 1) * MXU])
            act_ref[r, :, c * MXU:(c + 1) * MXU] = (a * jax.nn.sigmoid(a) * g).astype(BF16)
        y = _dot(act_ref[r], wout_ref[...])
        o_ref[rows, :] = x + _rms_norm(y, gpost_ref[...])


def _ffn(x, p):
    n = x.shape[0]
    return pl.pallas_call(
        _ffn_kernel,
        grid=(n // FFN_TILE,),
        in_specs=[
            pl.BlockSpec((FFN_TILE, D_MODEL), lambda i: (i, 0)),
            _const_spec((1, D_MODEL)),
            _const_spec((D_MODEL, 2 * D_FF)),
            _const_spec((D_FF, D_MODEL)),
            _const_spec((1, D_MODEL)),
        ],
        out_specs=pl.BlockSpec((FFN_TILE, D_MODEL), lambda i: (i, 0)),
        out_shape=jax.ShapeDtypeStruct((n, D_MODEL), F32),
        scratch_shapes=[pltpu.VMEM((FFN_TILE // SUB_TILE, SUB_TILE, D_FF), BF16)],
        compiler_params=pltpu.CompilerParams(
            dimension_semantics=("arbitrary",), vmem_limit_bytes=VMEM_LIMIT_BYTES),
        name="ffn",
    )(x, p["ffn_gpre"], p["ffn_win"], p["ffn_wout"], p["ffn_gpost"])


def _layer_params(l, norm_mix_pre, norm_mix_post, norm_ffn_pre, norm_ffn_post, w_in, conv_w,
                  pool_w, pool_scale, sg_norm, sg_w, sg_b, attn_sink, w_branch, w_gate, b_gate,
                  w_out, w_ffn_in, w_ffn_out):
    poolw = jnp.zeros((256, 256), F32)
    for g in range(256 // POOL_GDIM):
        poolw = poolw.at[g * POOL_GDIM:(g + 1) * POOL_GDIM,
                         g * POOL_GDIM:(g + 1) * POOL_GDIM].set(pool_w[l, g])
    return dict(
        w_in=w_in[l].astype(BF16),
        gpre=norm_mix_pre[l].reshape(1, D_MODEL),
        gpost=norm_mix_post[l].reshape(1, D_MODEL),
        convw=conv_w[l],
        poolw=poolw.astype(BF16),
        poolscale=pool_scale[l].reshape(1, 256),
        sgnorm=sg_norm[l].reshape(1, 256),
        sgwt=jnp.swapaxes(sg_w[l], 1, 2).astype(BF16),
        sgb=jnp.repeat(sg_b[l].T, SG_GDIM, axis=1),
        sink=attn_sink[l],
        wg=w_gate[l].astype(BF16),
        bg=b_gate[l],
        wb=w_branch[l].astype(BF16),
        wo=w_out[l].reshape(N_COL_CHUNKS, MXU, D_MODEL).astype(BF16),
        ffn_gpre=norm_ffn_pre[l].reshape(1, D_MODEL),
        ffn_gpost=norm_ffn_post[l].reshape(1, D_MODEL),
        ffn_win=w_ffn_in[l].astype(BF16),
        ffn_wout=w_ffn_out[l].astype(BF16),
    )


def _trunk(x, layers):
    b, s, d = x.shape
    x = x.reshape(b * s, d)
    for p in layers:
        h, z = _in_proj(x, p["gpre"], p["w_in"])
        x = _mix(x, h, z, p, s)
        x = _ffn(x, p)
    return x.reshape(b, s, d)


def kernel(x_prompt, x_sample, norm_mix_pre, norm_mix_post, norm_ffn_pre, norm_ffn_post, w_in,
           conv_w, pool_w, pool_scale, sg_norm, sg_w, sg_b, attn_sink, w_branch, w_gate, b_gate,
           w_out, w_ffn_in, w_ffn_out):
    depth = w_in.shape[0]
    layers = [
        _layer_params(l, norm_mix_pre, norm_mix_post, norm_ffn_pre, norm_ffn_post, w_in, conv_w,
                      pool_w, pool_scale, sg_norm, sg_w, sg_b, attn_sink, w_branch, w_gate,
                      b_gate, w_out, w_ffn_in, w_ffn_out)
        for l in range(depth)]
    return (_trunk(x_prompt, layers), _trunk(x_sample, layers))
```
